```python
import math
import jax, jax.numpy as jnp
from jax import lax
import numpy as np

D_MODEL = 1024
BATCH = 8
SEQ = 4096
DEPTH = 4

HEAD_DIM = 64
HEADS_PER_GROUP = 4
GROUP_WIDTH = HEADS_PER_GROUP * HEAD_DIM
N_MIXERS = 4
D_MIX = N_MIXERS * GROUP_WIDTH

MLA_Q_RANK = 192
MLA_KV_RANK = 128
MLA_NOPE_DIM = HEAD_DIM
MLA_ROPE_DIM = 32
MLA_V_DIM = HEAD_DIM
ROPE_THETA = 10000.0

Q_BLOCK = 128

MOBA_BLOCK = 256
MOBA_TOPK = 3
MOBA_Q_CHUNK = 32

CONV_WIDTH = 31

NUM_BUCKETS = 32
MAX_DISTANCE = 1024

NORM_EPS = 1e-6

COLS_MLA = MLA_Q_RANK + MLA_KV_RANK + MLA_ROPE_DIM
COLS_SB = 3 * GROUP_WIDTH
COLS_MOBA = 3 * GROUP_WIDTH
COLS_CONV = 2 * GROUP_WIDTH
COLS_GATE = D_MIX
D_IN_PROJ = COLS_MLA + COLS_SB + COLS_MOBA + COLS_CONV + COLS_GATE
SPLITS = [COLS_MLA, COLS_MLA + COLS_SB, COLS_MLA + COLS_SB + COLS_MOBA,
          COLS_MLA + COLS_SB + COLS_MOBA + COLS_CONV]

kernel_name = "hymba_mla_stickbreak_moba_conformer"


def rms_norm(x, g):
    x32 = x.astype(jnp.float32)
    y = x32 * lax.rsqrt(jnp.mean(x32 * x32, axis=-1, keepdims=True) + NORM_EPS)
    return (y * g.astype(jnp.float32)).astype(x.dtype)


def layer_norm(x, g, b):
    x32 = x.astype(jnp.float32)
    mu = jnp.mean(x32, axis=-1, keepdims=True)
    xc = x32 - mu
    var = jnp.mean(xc * xc, axis=-1, keepdims=True)
    y = xc * lax.rsqrt(var + NORM_EPS)
    return (y * g.astype(jnp.float32) + b.astype(jnp.float32)).astype(x.dtype)


def split_heads(t, n_heads):
    B, S, E = t.shape
    return t.reshape(B, S, n_heads, E // n_heads).transpose(0, 2, 1, 3)


def merge_heads(t):
    B, H, S, d = t.shape
    return t.transpose(0, 2, 1, 3).reshape(B, S, H * d)


def to_blocks(t, blk):
    B, H, S, d = t.shape
    return t.reshape(B, H, S // blk, blk, d).transpose(2, 0, 1, 3, 4)


def from_blocks(t):
    n, B, H, blk, d = t.shape
    return t.transpose(1, 2, 0, 3, 4).reshape(B, H, n * blk, d)


def rope_tables(pos):
    half = MLA_ROPE_DIM // 2
    freqs = ROPE_THETA ** (-jnp.arange(half, dtype=jnp.float32) / half)
    ang = pos.astype(jnp.float32)[:, None] * freqs[None, :]
    return jnp.cos(ang), jnp.sin(ang)


def apply_rope(x, cos, sin):
    half = x.shape[-1] // 2
    c = cos.astype(x.dtype)
    s = sin.astype(x.dtype)
    x1, x2 = x[..., :half], x[..., half:]
    return jnp.concatenate([x1 * c - x2 * s, x1 * s + x2 * c], axis=-1)


def t5_bucket(dist):
    n = jnp.maximum(dist, 0)
    max_exact = NUM_BUCKETS // 2
    n_large = jnp.maximum(n, max_exact).astype(jnp.float32)
    large = max_exact + (jnp.log(n_large / max_exact) / math.log(MAX_DISTANCE / max_exact)
                         * (NUM_BUCKETS - max_exact)).astype(jnp.int32)
    large = jnp.minimum(large, NUM_BUCKETS - 1)
    return jnp.where(n < max_exact, n, large)


def mla_attention(q_nope, q_rope, k_nope, k_rope, v):
    S = q_nope.shape[2]
    scale = (MLA_NOPE_DIM + MLA_ROPE_DIM) ** -0.5
    key_pos = jnp.arange(S, dtype=jnp.int32)

    def one_block(args):
        qn_b, qr_b, i = args
        s = (jnp.einsum('bhqd,bhkd->bhqk', qn_b, k_nope)
             + jnp.einsum('bhqd,bkd->bhqk', qr_b, k_rope)).astype(jnp.float32) * scale
        qpos = i * Q_BLOCK + jnp.arange(Q_BLOCK, dtype=jnp.int32)
        s = jnp.where(key_pos[None, :] <= qpos[:, None], s, -jnp.inf)
        p = jax.nn.softmax(s, axis=-1).astype(v.dtype)
        return jnp.einsum('bhqk,bhkd->bhqd', p, v)

    n = S // Q_BLOCK
    out = lax.map(one_block, (to_blocks(q_nope, Q_BLOCK), to_blocks(q_rope, Q_BLOCK),
                              jnp.arange(n, dtype=jnp.int32)))
    return from_blocks(out)


def stick_breaking_attention(q, k, v):
    S = q.shape[2]
    scale = q.shape[-1] ** -0.5
    key_pos = jnp.arange(S, dtype=jnp.int32)

    def one_block(args):
        q_b, i = args
        z = jnp.einsum('bhqd,bhkd->bhqk', q_b, k).astype(jnp.float32) * scale
        qpos = i * Q_BLOCK + jnp.arange(Q_BLOCK, dtype=jnp.int32)
        past = key_pos[None, :] < qpos[:, None]
        log_beta = jax.nn.log_sigmoid(z)
        log_1m_beta = jnp.where(past, jax.nn.log_sigmoid(-z), 0.0)
        suffix = lax.cumsum(log_1m_beta, axis=3, reverse=True) - log_1m_beta
        a = jnp.where(past, jnp.exp(log_beta + suffix), 0.0).astype(v.dtype)
        return jnp.einsum('bhqk,bhkd->bhqd', a, v)

    n = S // Q_BLOCK
    out = lax.map(one_block, (to_blocks(q, Q_BLOCK), jnp.arange(n, dtype=jnp.int32)))
    return from_blocks(out)


def moba_attention(q, k, v, rel_bias):
    B, H, S, d = q.shape
    nb = -(-S // MOBA_BLOCK)
    pad = nb * MOBA_BLOCK - S
    k_p = jnp.pad(k, ((0, 0), (0, 0), (0, pad), (0, 0)))
    v_p = jnp.pad(v, ((0, 0), (0, 0), (0, pad), (0, 0)))
    k_blocks = k_p.reshape(B, H, nb, MOBA_BLOCK, d)
    v_blocks = v_p.reshape(B, H, nb, MOBA_BLOCK, d)
    k_mean = jnp.mean(k_blocks.astype(jnp.float32), axis=3)
    top_k = min(MOBA_TOPK, nb)
    scale = d ** -0.5
    offs = jnp.arange(MOBA_BLOCK, dtype=jnp.int32)
    blk_ids = jnp.arange(nb, dtype=jnp.int32)
    bi = jnp.arange(B)[:, None, None, None]
    hi = jnp.arange(H)[None, :, None, None]
    bias_t = rel_bias.T

    def one_chunk(args):
        q_c, i = args
        start = i * MOBA_Q_CHUNK
        qpos = start + jnp.arange(MOBA_Q_CHUNK, dtype=jnp.int32)
        own_blk = start // MOBA_BLOCK
        gate = jnp.einsum('bhqd,bhnd->bhqn', q_c.astype(jnp.float32), k_mean)
        gate = jnp.where((blk_ids < own_blk)[None, :], gate, -jnp.inf)
        _, idx = lax.top_k(gate, top_k)
        valid = idx < own_blk
        k_sel = k_blocks[bi, hi, idx]
        v_sel = v_blocks[bi, hi, idx]
        s_sel = jnp.einsum('bhqd,bhqkcd->bhqkc', q_c, k_sel).astype(jnp.float32) * scale
        kpos_sel = idx[..., None] * MOBA_BLOCK + offs
        s_sel = s_sel + bias_t[hi[..., None], t5_bucket(qpos[:, None, None] - kpos_sel)]
        s_sel = jnp.where(valid[..., None], s_sel, -jnp.inf)
        s_sel = s_sel.reshape(B, H, MOBA_Q_CHUNK, top_k * MOBA_BLOCK)
        k_own = lax.dynamic_slice_in_dim(k_p, own_blk * MOBA_BLOCK, MOBA_BLOCK, axis=2)
        v_own = lax.dynamic_slice_in_dim(v_p, own_blk * MOBA_BLOCK, MOBA_BLOCK, axis=2)
        dist_own = qpos[:, None] - (own_blk * MOBA_BLOCK + offs)[None, :]
        s_own = (jnp.einsum('bhqd,bhcd->bhqc', q_c, k_own).astype(jnp.float32) * scale
                 + bias_t[:, t5_bucket(dist_own)])
        s_own = jnp.where(dist_own >= 0, s_own, -jnp.inf)
        p = jax.nn.softmax(jnp.concatenate([s_sel, s_own], axis=-1), axis=-1).astype(v.dtype)
        p_sel = p[..., :top_k * MOBA_BLOCK].reshape(B, H, MOBA_Q_CHUNK, top_k, MOBA_BLOCK)
        p_own = p[..., top_k * MOBA_BLOCK:]
        return (jnp.einsum('bhqkc,bhqkcd->bhqd', p_sel, v_sel)
                + jnp.einsum('bhqc,bhcd->bhqd', p_own, v_own))

    n = S // MOBA_Q_CHUNK
    out = lax.map(one_chunk, (to_blocks(q, MOBA_Q_CHUNK), jnp.arange(n, dtype=jnp.int32)))
    return from_blocks(out)


def conformer_conv(u, dw_w, dw_b, ln_g, ln_b, pw_w, pw_b):
    a, g = jnp.split(u, 2, axis=-1)
    xg = a * jax.nn.sigmoid(g)
    y = lax.conv_general_dilated(
        xg, dw_w[:, None, :].astype(xg.dtype), window_strides=(1,),
        padding=[(CONV_WIDTH - 1, 0)], dimension_numbers=('NWC', 'WIO', 'NWC'),
        feature_group_count=GROUP_WIDTH) + dw_b
    y = jax.nn.silu(layer_norm(y, ln_g, ln_b))
    return jnp.einsum('bsc,ce->bse', y, pw_w) + pw_b


def setup_inputs(seed: int = 0) -> dict:
    key = jax.random.key(seed)
    ks = jax.random.split(key, 16)
    f32 = jnp.float32

    def nrm(k, shape, fan_in):
        return jax.random.normal(k, shape, f32) * (fan_in ** -0.5)

    def gain(k, shape):
        return 1.0 + 0.05 * jax.random.normal(k, shape, f32)

    return {
        "x": jax.random.normal(ks[0], (BATCH, SEQ, D_MODEL), f32),
        "pre_norm_g": gain(ks[1], (DEPTH, D_MODEL)),
        "w_in": nrm(ks[2], (DEPTH, D_MODEL, D_IN_PROJ), D_MODEL),
        "mla_q_norm_g": gain(ks[3], (DEPTH, MLA_Q_RANK)),
        "mla_w_uq": nrm(ks[4], (DEPTH, MLA_Q_RANK, HEADS_PER_GROUP * (MLA_NOPE_DIM + MLA_ROPE_DIM)), MLA_Q_RANK),
        "mla_kv_norm_g": gain(ks[5], (DEPTH, MLA_KV_RANK)),
        "mla_w_ukv": nrm(ks[6], (DEPTH, MLA_KV_RANK, HEADS_PER_GROUP * (MLA_NOPE_DIM + MLA_V_DIM)), MLA_KV_RANK),
        "rel_bias": 0.2 * jax.random.normal(ks[7], (NUM_BUCKETS, HEADS_PER_GROUP), f32),
        "conv_dw_w": nrm(ks[8], (DEPTH, CONV_WIDTH, GROUP_WIDTH), CONV_WIDTH),
        "conv_dw_b": 0.02 * jax.random.normal(ks[9], (DEPTH, GROUP_WIDTH), f32),
        "conv_ln_g": gain(ks[10], (DEPTH, GROUP_WIDTH)),
        "conv_ln_b": 0.02 * jax.random.normal(ks[11], (DEPTH, GROUP_WIDTH), f32),
        "conv_pw_w": nrm(ks[12], (DEPTH, GROUP_WIDTH, GROUP_WIDTH), GROUP_WIDTH),
        "conv_pw_b": 0.02 * jax.random.normal(ks[13], (DEPTH, GROUP_WIDTH), f32),
        "w_out": nrm(ks[14], (DEPTH, D_MIX, D_MODEL), D_MIX),
        "post_norm_g": gain(ks[15], (DEPTH, D_MODEL)),
    }


def reference(x, pre_norm_g, w_in, mla_q_norm_g, mla_w_uq, mla_kv_norm_g, mla_w_ukv,
              rel_bias, conv_dw_w, conv_dw_b, conv_ln_g, conv_ln_b, conv_pw_w, conv_pw_b,
              w_out, post_norm_g):
    S = x.shape[1]
    cos, sin = rope_tables(jnp.arange(S, dtype=jnp.int32))
    H = HEADS_PER_GROUP
    for l in range(DEPTH):
        h = rms_norm(x, pre_norm_g[l])
        u = jnp.einsum('bsd,de->bse', h, w_in[l])
        u_mla, u_sb, u_moba, u_conv, u_gate = jnp.split(u, SPLITS, axis=-1)

        c_q, c_kv, k_rope_raw = jnp.split(u_mla, [MLA_Q_RANK, MLA_Q_RANK + MLA_KV_RANK], axis=-1)
        qa = split_heads(jnp.einsum('bsr,re->bse', rms_norm(c_q, mla_q_norm_g[l]), mla_w_uq[l]), H)
        q_nope = qa[..., :MLA_NOPE_DIM]
        q_rope = apply_rope(qa[..., MLA_NOPE_DIM:], cos, sin)
        kva = split_heads(jnp.einsum('bsr,re->bse', rms_norm(c_kv, mla_kv_norm_g[l]), mla_w_ukv[l]), H)
        k_nope = kva[..., :MLA_NOPE_DIM]
        v_a = kva[..., MLA_NOPE_DIM:]
        k_rope = apply_rope(k_rope_raw, cos, sin)
        o_a = merge_heads(mla_attention(q_nope, q_rope, k_nope, k_rope, v_a))

        qb, kb, vb = jnp.split(u_sb, 3, axis=-1)
        o_b = merge_heads(stick_breaking_attention(split_heads(qb, H), split_heads(kb, H), split_heads(vb, H)))

        qc, kc, vc = jnp.split(u_moba, 3, axis=-1)
        o_c = merge_heads(moba_attention(split_heads(qc, H), split_heads(kc, H), split_heads(vc, H), rel_bias))

        o_d = conformer_conv(u_conv, conv_dw_w[l], conv_dw_b[l], conv_ln_g[l], conv_ln_b[l],
                             conv_pw_w[l], conv_pw_b[l])

        y = jnp.concatenate([o_a, o_b, o_c, o_d], axis=-1) * jax.nn.silu(u_gate)
        y = jnp.einsum('bse,ed->bsd', y, w_out[l])
        x = x + rms_norm(y, post_norm_g[l])
    return x
```

```python
import functools
import math

import jax
import jax.numpy as jnp
from jax import lax
from jax.experimental import pallas as pl
from jax.experimental.pallas import tpu as pltpu

F32 = jnp.float32
BF16 = jnp.bfloat16

D_MODEL = 1024
HEAD_DIM = 64
N_HEADS = 4
GROUP = N_HEADS * HEAD_DIM
D_MIX = 4 * GROUP
Q_RANK = 192
KV_RANK = 128
ROPE_DIM = 32
ROPE_HALF = ROPE_DIM // 2
ROPE_THETA = 10000.0
MLA_HEAD_PAD = 128
CONV_WIDTH = 31
NUM_BUCKETS = 32
MAX_DISTANCE = 1024
MOBA_TOPK = 3
EPS = 1e-6

TILE = 256
ROW_TILE = 512
CONV_HALO = 32
N_BIAS_TILES = 6
MLA_U = 640
MAIN_COLS = MLA_U + 4 * GROUP + 2 * GROUP + D_MIX
VMEM_LIMIT = 56 * 1024 * 1024

_NT = (((1,), (1,)), ((), ()))


def _params(n_axes):
    return pltpu.CompilerParams(dimension_semantics=("arbitrary",) * n_axes,
                                vmem_limit_bytes=VMEM_LIMIT)


def _dot(a, b):
    return jnp.dot(a, b, preferred_element_type=F32)


def _dot_nt(a, b):
    return lax.dot_general(a, b, _NT, preferred_element_type=F32)


def _bias_tiles_kernel(rb_ref, o_ref):
    d = pl.program_id(0)
    row = lax.broadcasted_iota(jnp.int32, (TILE, TILE), 0)
    col = lax.broadcasted_iota(jnp.int32, (TILE, TILE), 1)
    n = jnp.maximum(d * TILE + col - row, 0)
    max_exact = NUM_BUCKETS // 2
    n_large = jnp.maximum(n, max_exact).astype(F32)
    large = max_exact + (jnp.log(n_large / max_exact) / math.log(MAX_DISTANCE / max_exact)
                         * (NUM_BUCKETS - max_exact)).astype(jnp.int32)
    large = jnp.minimum(large, NUM_BUCKETS - 1)
    bucket = jnp.where(n < max_exact, n, large)
    for h in range(N_HEADS):
        acc = jnp.zeros((TILE, TILE), F32)
        for b in range(NUM_BUCKETS):
            acc = jnp.where(bucket == b, rb_ref[b, h], acc)
        o_ref[0, h] = acc


def _bias_tiles(rel_bias):
    return pl.pallas_call(
        _bias_tiles_kernel,
        grid=(N_BIAS_TILES,),
        in_specs=[pl.BlockSpec(memory_space=pltpu.SMEM)],
        out_specs=pl.BlockSpec((1, N_HEADS, TILE, TILE), lambda d: (d, 0, 0, 0)),
        out_shape=jax.ShapeDtypeStruct((N_BIAS_TILES, N_HEADS, TILE, TILE), F32),
        compiler_params=_params(1),
        name="t5_bias_tiles",
    )(rel_bias)


def _in_proj_kernel(x_ref, g_ref, w_ref, wvt_ref,
                    mla_ref, sbq_ref, sbk_ref, mbq_ref, mbqf_ref, mbk_ref, conv_ref, gate_ref,
                    sbvt_ref, mbvt_ref, kmean_ref):
    x = x_ref[...]
    ms = jnp.mean(x * x, axis=-1, keepdims=True)
    h = (x * lax.rsqrt(ms + EPS) * g_ref[...]).astype(BF16)

    def mm(lo, hi):
        return _dot(h, w_ref[:, lo:hi])

    c = 0
    mla_ref[...] = mm(c, c + MLA_U); c += MLA_U
    sbq_ref[...] = mm(c, c + GROUP).astype(BF16); c += GROUP
    sbk_ref[...] = mm(c, c + GROUP).astype(BF16); c += GROUP
    qf = mm(c, c + GROUP); c += GROUP
    mbqf_ref[...] = qf
    mbq_ref[...] = qf.astype(BF16)
    kf = mm(c, c + GROUP); c += GROUP
    mbk_ref[...] = kf.astype(BF16)
    conv_ref[...] = mm(c, c + 2 * GROUP); c += 2 * GROUP
    gate_ref[...] = mm(c, c + D_MIX)
    vt = _dot_nt(wvt_ref[...], h)
    for t in range(ROW_TILE // TILE):
        rows = slice(t * TILE, (t + 1) * TILE)
        kmean_ref[0, t] = jnp.mean(kf[rows], axis=0, keepdims=True)
        sbvt_ref[0, t] = vt[:GROUP, rows].astype(BF16)
        mbvt_ref[0, t] = vt[GROUP:, rows].astype(BF16)


def _in_proj(x2, g, w_main, w_vt, batch, seq):
    n = batch * seq
    nk = seq // TILE
    per_b = seq // ROW_TILE
    tpr = ROW_TILE // TILE
    row = lambda cols: pl.BlockSpec((ROW_TILE, cols), lambda i: (i, 0))
    full = lambda shp: pl.BlockSpec(shp, lambda i: (0,) * len(shp))
    vt_spec = pl.BlockSpec((1, tpr, GROUP, TILE), lambda i: (i // per_b, i % per_b, 0, 0))
    km_spec = pl.BlockSpec((1, tpr, 1, GROUP), lambda i: (i // per_b, i % per_b, 0, 0))
    sd = jax.ShapeDtypeStruct
    return pl.pallas_call(
        _in_proj_kernel,
        grid=(n // ROW_TILE,),
        in_specs=[row(D_MODEL), full((1, D_MODEL)), full((D_MODEL, MAIN_COLS)),
                  full((2 * GROUP, D_MODEL))],
        out_specs=[row(MLA_U), row(GROUP), row(GROUP), row(GROUP), row(GROUP), row(GROUP),
                   row(2 * GROUP), row(D_MIX), vt_spec, vt_spec, km_spec],
        out_shape=[sd((n, MLA_U), F32), sd((n, GROUP), BF16), sd((n, GROUP), BF16),
                   sd((n, GROUP), BF16), sd((n, GROUP), F32), sd((n, GROUP), BF16),
                   sd((n, 2 * GROUP), F32), sd((n, D_MIX), F32),
                   sd((batch, nk, GROUP, TILE), BF16), sd((batch, nk, GROUP, TILE), BF16),
                   sd((batch, nk, 1, GROUP), F32)],
        compiler_params=_params(1),
        name="in_proj",
    )(x2, g, w_main, w_vt)


def _mla_prep_kernel(u_ref, gq_ref, gkv_ref, wq_ref, wqs_ref, wkn_ref, wvt_ref,
                     cq_ref, sq_ref, ck_ref, sk_ref, q_ref, k_ref, vt_ref):
    u = u_ref[...]
    cq = u[:, 0:256]
    msq = jnp.sum(cq * cq, axis=-1, keepdims=True) * (1.0 / Q_RANK)
    cqn = (cq * lax.rsqrt(msq + EPS) * gq_ref[...]).astype(BF16)
    qa = _dot(cqn, wq_ref[...])
    qs = _dot(cqn, wqs_ref[...])
    ckv = u[:, 256:384]
    msk = jnp.mean(ckv * ckv, axis=-1, keepdims=True)
    ckvn = (ckv * lax.rsqrt(msk + EPS) * gkv_ref[...]).astype(BF16)
    kn = _dot(ckvn, wkn_ref[...])
    kr = u[:, 384:512] * ck_ref[...] + u[:, 512:640] * sk_ref[...]
    cq_t = cq_ref[...]
    sq_t = sq_ref[...]
    for h in range(N_HEADS):
        sl = slice(h * MLA_HEAD_PAD, (h + 1) * MLA_HEAD_PAD)
        q_ref[0, h] = (qa[:, sl] * cq_t + qs[:, sl] * sq_t).astype(BF16)
        k_ref[0, h] = (kn[:, sl] + kr).astype(BF16)
    vt = _dot_nt(wvt_ref[...], ckvn)
    for t in range(ROW_TILE // TILE):
        vt_ref[0, t] = vt[:, t * TILE:(t + 1) * TILE].astype(BF16)


def _mla_prep(mla_u, gq, gkv, wq, wqs, wkn, wvt, cq, sq, ck, sk, batch, seq):
    nk = seq // TILE
    per_b = seq // ROW_TILE
    tpr = ROW_TILE // TILE
    full = lambda shp: pl.BlockSpec(shp, lambda b, i: (0,) * len(shp))
    tab = pl.BlockSpec((ROW_TILE, MLA_HEAD_PAD), lambda b, i: (i, 0))
    head_spec = pl.BlockSpec((1, N_HEADS, ROW_TILE, MLA_HEAD_PAD), lambda b, i: (b, 0, i, 0))
    sd = jax.ShapeDtypeStruct
    return pl.pallas_call(
        _mla_prep_kernel,
        grid=(batch, per_b),
        in_specs=[pl.BlockSpec((ROW_TILE, MLA_U), lambda b, i: (b * per_b + i, 0)),
                  full((1, 256)), full((1, KV_RANK)),
                  full((256, N_HEADS * MLA_HEAD_PAD)), full((256, N_HEADS * MLA_HEAD_PAD)),
                  full((KV_RANK, N_HEADS * MLA_HEAD_PAD)), full((GROUP, KV_RANK)),
                  tab, tab, tab, tab],
        out_specs=[head_spec, head_spec,
                   pl.BlockSpec((1, tpr, GROUP, TILE), lambda b, i: (b, i, 0, 0))],
        out_shape=[sd((batch, N_HEADS, seq, MLA_HEAD_PAD), BF16),
                   sd((batch, N_HEADS, seq, MLA_HEAD_PAD), BF16),
                   sd((batch, nk, GROUP, TILE), BF16)],
        compiler_params=_params(2),
        name="mla_prep",
    )(mla_u, gq, gkv, wq, wqs, wkn, wvt, cq, sq, ck, sk)


def _softmax_step(s, vt, m, l, acc):
    m_new = jnp.maximum(m, jnp.max(s, axis=0, keepdims=True))
    alpha = jnp.exp(m - m_new)
    p = jnp.exp(s - m_new)
    l_new = alpha * l + jnp.sum(p, axis=0, keepdims=True)
    acc_new = alpha * acc + _dot(vt, p.astype(BF16))
    return m_new, l_new, acc_new


def _head_lane_mask(h):
    lane = lax.broadcasted_iota(jnp.int32, (TILE, GROUP), 1)
    return (lane >= h * HEAD_DIM) & (lane < (h + 1) * HEAD_DIM)


def _mla_attn_kernel(q_ref, k_ref, vt_ref, o_ref, acc_ref):
    qi = pl.program_id(1)
    row = lax.broadcasted_iota(jnp.int32, (TILE, TILE), 0)
    col = lax.broadcasted_iota(jnp.int32, (TILE, TILE), 1)
    for h in range(N_HEADS):
        q = q_ref[0, h]
        hs = slice(h * HEAD_DIM, (h + 1) * HEAD_DIM)

        def scores(j):
            k = k_ref[0, h, pl.ds(pl.multiple_of(j * TILE, TILE), TILE), :]
            return _dot_nt(k, q)

        s = jnp.where(row <= col, scores(qi), -jnp.inf)
        state = _softmax_step(s, vt_ref[0, qi, hs, :],
                              jnp.full((1, TILE), -jnp.inf, F32), jnp.zeros((1, TILE), F32),
                              jnp.zeros((HEAD_DIM, TILE), F32))

        def body(j, st):
            return _softmax_step(scores(j), vt_ref[0, j, hs, :], *st)

        m, l, acc = lax.fori_loop(0, qi, body, state)
        acc_ref[hs, :] = acc / l
    o_ref[0] = acc_ref[...].T


def _mla_attn(q, k, vt, batch, seq):
    nk = seq // TILE
    return pl.pallas_call(
        _mla_attn_kernel,
        grid=(batch, nk),
        in_specs=[pl.BlockSpec((1, N_HEADS, TILE, MLA_HEAD_PAD), lambda b, i: (b, 0, i, 0)),
                  pl.BlockSpec((1, N_HEADS, seq, MLA_HEAD_PAD), lambda b, i: (b, 0, 0, 0)),
                  pl.BlockSpec((1, nk, GROUP, TILE), lambda b, i: (b, 0, 0, 0))],
        out_specs=pl.BlockSpec((1, TILE, GROUP), lambda b, i: (b, i, 0)),
        out_shape=jax.ShapeDtypeStruct((batch, seq, GROUP), F32),
        scratch_shapes=[pltpu.VMEM((GROUP, TILE), F32)],
        compiler_params=_params(2),
        name="mla_attn",
    )(q, k, vt)


def _split_bf16(x):
    hi = x.astype(BF16)
    lo = (x - hi.astype(F32)).astype(BF16)
    return hi, lo


def _sb_attn_kernel(q_ref, k_ref, vt_ref, o_ref, acc_ref):
    qi = pl.program_id(1)
    q = q_ref[0]
    row = lax.broadcasted_iota(jnp.int32, (TILE, TILE), 0)
    col = lax.broadcasted_iota(jnp.int32, (TILE, TILE), 1)
    past = row < col
    after = (col > row).astype(BF16)
    for h in range(N_HEADS):
        qh = jnp.where(_head_lane_mask(h), q, jnp.zeros_like(q))
        hs = slice(h * HEAD_DIM, (h + 1) * HEAD_DIM)

        def tile(j, carry, acc, diag):
            k = k_ref[0, pl.ds(pl.multiple_of(j * TILE, TILE), TILE), :]
            z = _dot_nt(k, qh)
            sp = jnp.log1p(jnp.exp(-jnp.abs(z)))
            log_beta = jnp.minimum(z, 0.0) - sp
            log_1m = log_beta - z
            if diag:
                log_1m = jnp.where(past, log_1m, 0.0)
            hi, lo = _split_bf16(log_1m)
            suffix = _dot(after, hi) + _dot(after, lo)
            a = jnp.exp(log_beta + suffix + carry)
            if diag:
                a = jnp.where(past, a, 0.0)
            acc = acc + _dot(vt_ref[0, j, hs, :], a.astype(BF16))
            carry = carry + suffix[0:1, :] + log_1m[0:1, :]
            return carry, acc

        st = tile(qi, jnp.zeros((1, TILE), F32), jnp.zeros((HEAD_DIM, TILE), F32), True)

        def body(t, st):
            return tile(qi - 1 - t, st[0], st[1], False)

        _, acc = lax.fori_loop(0, qi, body, st)
        acc_ref[hs, :] = acc
    o_ref[0] = acc_ref[...].T


def _sb_attn(q, k, vt, batch, seq):
    nk = seq // TILE
    return pl.pallas_call(
        _sb_attn_kernel,
        grid=(batch, nk),
        in_specs=[pl.BlockSpec((1, TILE, GROUP), lambda b, i: (b, i, 0)),
                  pl.BlockSpec((1, seq, GROUP), lambda b, i: (b, 0, 0)),
                  pl.BlockSpec((1, nk, GROUP, TILE), lambda b, i: (b, 0, 0, 0))],
        out_specs=pl.BlockSpec((1, TILE, GROUP), lambda b, i: (b, i, 0)),
        out_shape=jax.ShapeDtypeStruct((batch, seq, GROUP), F32),
        scratch_shapes=[pltpu.VMEM((GROUP, TILE), F32)],
        compiler_params=_params(2),
        name="sb_attn",
    )(q, k, vt)


def _moba_attn_kernel(nk, q_ref, qf_ref, k_ref, vt_ref, km_ref, bias_ref, o_ref, acc_ref, sel_ref):
    qi = pl.program_id(1)
    q = q_ref[0]
    qf = qf_ref[0]
    km_hi, km_lo = _split_bf16(km_ref[0, :, 0, :])
    row = lax.broadcasted_iota(jnp.int32, (TILE, TILE), 0)
    col = lax.broadcasted_iota(jnp.int32, (TILE, TILE), 1)
    blk = lax.broadcasted_iota(jnp.int32, (nk, TILE), 0)
    for h in range(N_HEADS):
        hm = _head_lane_mask(h)
        qh = jnp.where(hm, q, jnp.zeros_like(q))
        hs = slice(h * HEAD_DIM, (h + 1) * HEAD_DIM)

        qf_hi, qf_lo = _split_bf16(jnp.where(hm, qf, 0.0))
        gate = _dot_nt(km_hi, qf_hi) + (_dot_nt(km_hi, qf_lo) + _dot_nt(km_lo, qf_hi))
        beaten = jnp.zeros((nk, TILE), jnp.int32)
        for jp in range(nk):
            gj = gate[jp:jp + 1, :]
            beats = (gj > gate) | ((gj == gate) & (jp < blk))
            beaten = beaten + jnp.where(beats & (jp < qi), 1, 0)
        keep = (blk < qi) & (beaten < MOBA_TOPK)
        sel_ref[h] = jnp.where(keep, 0.0, -jnp.inf)

        def scores(j):
            k = k_ref[0, pl.ds(pl.multiple_of(j * TILE, TILE), TILE), :]
            return _dot_nt(k, qh)

        s = jnp.where(row <= col, scores(qi) + bias_ref[0, h], -jnp.inf)
        state = _softmax_step(s, vt_ref[0, qi, hs, :],
                              jnp.full((1, TILE), -jnp.inf, F32), jnp.zeros((1, TILE), F32),
                              jnp.zeros((HEAD_DIM, TILE), F32))

        def body(j, st):
            d = jnp.minimum(qi - j, N_BIAS_TILES - 1)
            s = scores(j) + bias_ref[d, h] + sel_ref[h, pl.ds(j, 1), :]
            return _softmax_step(s, vt_ref[0, j, hs, :], *st)

        m, l, acc = lax.fori_loop(0, qi, body, state)
        acc_ref[hs, :] = acc / l
    o_ref[0] = acc_ref[...].T


def _moba_attn(q, qf, k, vt, kmean, bias, batch, seq):
    nk = seq // TILE
    return pl.pallas_call(
        functools.partial(_moba_attn_kernel, nk),
        grid=(batch, nk),
        in_specs=[pl.BlockSpec((1, TILE, GROUP), lambda b, i: (b, i, 0)),
                  pl.BlockSpec((1, TILE, GROUP), lambda b, i: (b, i, 0)),
                  pl.BlockSpec((1, seq, GROUP), lambda b, i: (b, 0, 0)),
                  pl.BlockSpec((1, nk, GROUP, TILE), lambda b, i: (b, 0, 0, 0)),
                  pl.BlockSpec((1, nk, 1, GROUP), lambda b, i: (b, 0, 0, 0)),
                  pl.BlockSpec((N_BIAS_TILES, N_HEADS, TILE, TILE), lambda b, i: (0, 0, 0, 0))],
        out_specs=pl.BlockSpec((1, TILE, GROUP), lambda b, i: (b, i, 0)),
        out_shape=jax.ShapeDtypeStruct((batch, seq, GROUP), F32),
        scratch_shapes=[pltpu.VMEM((GROUP, TILE), F32), pltpu.VMEM((N_HEADS, nk, TILE), F32)],
        compiler_params=_params(2),
        name="moba_attn",
    )(q, qf, k, vt, kmean, bias)


def _conv_kernel(u_ref, halo_ref, dw_ref, dwb_ref, lng_ref, lnb_ref, pw_ref, pwb_ref, o_ref, ext_ref):
    i = pl.program_id(1)
    um = u_ref[0]
    uh = halo_ref[0]
    xh = uh[:, :GROUP] * jax.nn.sigmoid(uh[:, GROUP:])
    ext_ref[0:CONV_HALO, :] = jnp.where(i == 0, 0.0, xh)
    ext_ref[CONV_HALO:, :] = um[:, :GROUP] * jax.nn.sigmoid(um[:, GROUP:])
    base = CONV_HALO - (CONV_WIDTH - 1)
    y = jnp.zeros((ROW_TILE, GROUP), F32)
    for w in range(CONV_WIDTH):
        y = y + ext_ref[base + w:base + w + ROW_TILE, :] * dw_ref[w:w + 1, :]
    y = y + dwb_ref[...]
    mu = jnp.mean(y, axis=-1, keepdims=True)
    yc = y - mu
    var = jnp.mean(yc * yc, axis=-1, keepdims=True)
    yn = yc * lax.rsqrt(var + EPS) * lng_ref[...] + lnb_ref[...]
    sw = yn * jax.nn.sigmoid(yn)
    o_ref[0] = _dot(sw.astype(BF16), pw_ref[...]) + pwb_ref[...]


def _conv_module(u_conv, dw, dwb, lng, lnb, pw, pwb, batch, seq):
    per_b = seq // ROW_TILE
    halo_per_tile = ROW_TILE // CONV_HALO
    full = lambda shp: pl.BlockSpec(shp, lambda b, i: (0,) * len(shp))
    return pl.pallas_call(
        _conv_kernel,
        grid=(batch, per_b),
        in_specs=[pl.BlockSpec((1, ROW_TILE, 2 * GROUP), lambda b, i: (b, i, 0)),
                  pl.BlockSpec((1, CONV_HALO, 2 * GROUP),
                               lambda b, i: (b, jnp.maximum(i * halo_per_tile - 1, 0), 0)),
                  full((CONV_HALO, GROUP)), full((1, GROUP)), full((1, GROUP)), full((1, GROUP)),
                  full((GROUP, GROUP)), full((1, GROUP))],
        out_specs=pl.BlockSpec((1, ROW_TILE, GROUP), lambda b, i: (b, i, 0)),
        out_shape=jax.ShapeDtypeStruct((batch, seq, GROUP), F32),
        scratch_shapes=[pltpu.VMEM((CONV_HALO + ROW_TILE, GROUP), F32)],
        compiler_params=_params(2),
        name="conv_module",
    )(u_conv, u_conv, dw, dwb, lng, lnb, pw, pwb)


def _out_proj_kernel(oa_ref, ob_ref, oc_ref, od_ref, gate_ref, w_ref, g_ref, x_ref, o_ref):
    gt = gate_ref[...]
    sg = gt * jax.nn.sigmoid(gt)
    mix = jnp.concatenate([oa_ref[...], ob_ref[...], oc_ref[...], od_ref[...]], axis=-1)
    y = _dot((mix * sg).astype(BF16), w_ref[...])
    ms = jnp.mean(y * y, axis=-1, keepdims=True)
    o_ref[...] = x_ref[...] + y * lax.rsqrt(ms + EPS) * g_ref[...]


def _out_proj(oa, ob, oc, od, gate, w, g, x2):
    n = x2.shape[0]
    row = lambda cols: pl.BlockSpec((ROW_TILE, cols), lambda i: (i, 0))
    full = lambda shp: pl.BlockSpec(shp, lambda i: (0,) * len(shp))
    return pl.pallas_call(
        _out_proj_kernel,
        grid=(n // ROW_TILE,),
        in_specs=[row(GROUP), row(GROUP), row(GROUP), row(GROUP), row(D_MIX),
                  full((D_MIX, D_MODEL)), full((1, D_MODEL)), row(D_MODEL)],
        out_specs=row(D_MODEL),
        out_shape=jax.ShapeDtypeStruct((n, D_MODEL), F32),
        compiler_params=_params(1),
        name="out_proj",
    )(oa, ob, oc, od, gate, w, g, x2)


def _prep_in_proj_weights(w_in):
    depth = w_in.shape[0]
    z = lambda cols: jnp.zeros((depth, D_MODEL, cols), w_in.dtype)
    c = 0
    cq = w_in[..., c:c + Q_RANK]; c += Q_RANK
    ckv = w_in[..., c:c + KV_RANK]; c += KV_RANK
    kr = w_in[..., c:c + ROPE_DIM]; c += ROPE_DIM
    sbq, sbk, sbv = (w_in[..., c + i * GROUP:c + (i + 1) * GROUP] for i in range(3)); c += 3 * GROUP
    mbq, mbk, mbv = (w_in[..., c + i * GROUP:c + (i + 1) * GROUP] for i in range(3)); c += 3 * GROUP
    conv = w_in[..., c:c + 2 * GROUP]; c += 2 * GROUP
    gate = w_in[..., c:c + D_MIX]
    kr_swapped = jnp.concatenate([kr[..., ROPE_HALF:], kr[..., :ROPE_HALF]], axis=-1)
    pad_rope = lambda r: jnp.concatenate([z(HEAD_DIM), r, z(MLA_HEAD_PAD - HEAD_DIM - ROPE_DIM)], axis=-1)
    scale = HEAD_DIM ** -0.5
    main = jnp.concatenate([cq, z(256 - Q_RANK), ckv, pad_rope(kr), pad_rope(kr_swapped),
                            sbq * scale, sbk, mbq * scale, mbk, conv, gate], axis=-1)
    vt = jnp.concatenate([jnp.swapaxes(sbv, 1, 2), jnp.swapaxes(mbv, 1, 2)], axis=1)
    return main.astype(BF16), vt.astype(BF16)


def _prep_mla_weights(w_uq, w_ukv):
    depth = w_uq.shape[0]
    qh = w_uq.reshape(depth, Q_RANK, N_HEADS, HEAD_DIM + ROPE_DIM)
    nope, r1, r2 = qh[..., :HEAD_DIM], qh[..., HEAD_DIM:HEAD_DIM + ROPE_HALF], qh[..., HEAD_DIM + ROPE_HALF:]
    zpad = jnp.zeros((depth, Q_RANK, N_HEADS, MLA_HEAD_PAD - HEAD_DIM - ROPE_DIM), w_uq.dtype)
    row_pad = ((0, 0), (0, 256 - Q_RANK), (0, 0))
    wq = jnp.pad(jnp.concatenate([nope, r1, r2, zpad], -1).reshape(depth, Q_RANK, -1), row_pad)
    wqs = jnp.pad(jnp.concatenate([jnp.zeros_like(nope), r2, r1, zpad], -1).reshape(depth, Q_RANK, -1), row_pad)
    kvh = w_ukv.reshape(depth, KV_RANK, N_HEADS, 2 * HEAD_DIM)
    k_nope, v = kvh[..., :HEAD_DIM], kvh[..., HEAD_DIM:]
    wkn = jnp.concatenate([k_nope, jnp.zeros_like(k_nope)], -1).reshape(depth, KV_RANK, -1)
    wvt = jnp.swapaxes(v.reshape(depth, KV_RANK, GROUP), 1, 2)
    return wq.astype(BF16), wqs.astype(BF16), wkn.astype(BF16), wvt.astype(BF16)


def _rope_tables(seq):
    freqs = ROPE_THETA ** (-jnp.arange(ROPE_HALF, dtype=F32) / ROPE_HALF)
    ang = jnp.arange(seq, dtype=jnp.int32).astype(F32)[:, None] * freqs[None, :]
    cos, sin = jnp.cos(ang), jnp.sin(ang)
    ones, zeros = jnp.ones((seq, HEAD_DIM), F32), jnp.zeros((seq, HEAD_DIM), F32)
    ztail = jnp.zeros((seq, MLA_HEAD_PAD - HEAD_DIM - ROPE_DIM), F32)
    c_tab = jnp.concatenate([ones, cos, cos, ztail], axis=-1)
    s_tab = jnp.concatenate([zeros, -sin, sin, ztail], axis=-1)
    return c_tab, s_tab


def kernel(x, pre_norm_g, w_in, mla_q_norm_g, mla_w_uq, mla_kv_norm_g, mla_w_ukv, rel_bias, conv_dw_w, conv_dw_b, conv_ln_g, conv_ln_b, conv_pw_w, conv_pw_b, w_out, post_norm_g):
    batch, seq, d_model = x.shape
    depth = w_in.shape[0]
    assert d_model == D_MODEL and seq % ROW_TILE == 0 and ROW_TILE % TILE == 0
    n = batch * seq

    w_main, w_vt = _prep_in_proj_weights(w_in)
    wq, wqs, wkn, wvt_mla = _prep_mla_weights(mla_w_uq, mla_w_ukv)
    c_tab, s_tab = _rope_tables(seq)
    mla_scale = (HEAD_DIM + ROPE_DIM) ** -0.5
    cq_tab, sq_tab = c_tab * mla_scale, s_tab * mla_scale
    gq = jnp.pad(mla_q_norm_g, ((0, 0), (0, 256 - Q_RANK)))
    dw = jnp.pad(conv_dw_w, ((0, 0), (0, CONV_HALO - CONV_WIDTH), (0, 0)))
    pw = conv_pw_w.astype(BF16)
    w_o = w_out.astype(BF16)
    bias = _bias_tiles(rel_bias)

    x2 = x.reshape(n, D_MODEL)
    for l in range(depth):
        (mla_u, sbq, sbk, mbq, mbqf, mbk, u_conv, u_gate, sbvt, mbvt, kmean) = _in_proj(
            x2, pre_norm_g[l][None], w_main[l], w_vt[l], batch, seq)
        qcat, kcat, mla_vt = _mla_prep(mla_u, gq[l][None], mla_kv_norm_g[l][None], wq[l], wqs[l], wkn[l],
                                       wvt_mla[l], cq_tab, sq_tab, c_tab, s_tab, batch, seq)
        o_a = _mla_attn(qcat, kcat, mla_vt, batch, seq)
        o_b = _sb_attn(sbq.reshape(batch, seq, GROUP), sbk.reshape(batch, seq, GROUP), sbvt, batch, seq)
        o_c = _moba_attn(mbq.reshape(batch, seq, GROUP), mbqf.reshape(batch, seq, GROUP),
                         mbk.reshape(batch, seq, GROUP), mbvt, kmean, bias, batch, seq)
        o_d = _conv_module(u_conv.reshape(batch, seq, 2 * GROUP), dw[l], conv_dw_b[l][None],
                           conv_ln_g[l][None], conv_ln_b[l][None], pw[l], conv_pw_b[l][None], batch, seq)
        x2 = _out_proj(o_a.reshape(n, GROUP), o_b.reshape(n, GROUP), o_c.reshape(n, GROUP),
                       o_d.reshape(n, GROUP), u_gate, w_o[l], post_norm_g[l][None], x2)
    return x2.reshape(batch, seq, D_MODEL)
```

```python
import functools
import math

import jax
import jax.numpy as jnp
from jax import lax
from jax.experimental import pallas as pl
from jax.experimental.pallas import tpu as pltpu

F32 = jnp.float32
BF16 = jnp.bfloat16

D_MODEL = 1024
HEAD_DIM = 64
N_HEADS = 4
GROUP = N_HEADS * HEAD_DIM
D_MIX = 4 * GROUP
Q_RANK = 192
KV_RANK = 128
ROPE_DIM = 32
ROPE_HALF = ROPE_DIM // 2
ROPE_THETA = 10000.0
MLA_HEAD_PAD = 128
CONV_WIDTH = 31
NUM_BUCKETS = 32
MAX_DISTANCE = 1024
MOBA_TOPK = 3
EPS = 1e-6

TILE = 256
ROW_TILE = 512
CONV_HALO = 32
N_BIAS_TILES = 6
MLA_U = 640
MAIN_COLS = MLA_U + 4 * GROUP + 2 * GROUP + D_MIX
VMEM_LIMIT = 56 * 1024 * 1024
SB_EXP_UNDERFLOW = -104.0

_NT = (((1,), (1,)), ((), ()))


def _params(n_axes):
    return pltpu.CompilerParams(dimension_semantics=("arbitrary",) * n_axes,
                                vmem_limit_bytes=VMEM_LIMIT)


def _dot(a, b):
    return jnp.dot(a, b, preferred_element_type=F32)


def _dot_nt(a, b):
    return lax.dot_general(a, b, _NT, preferred_element_type=F32)


def _bias_tiles_kernel(rb_ref, o_ref):
    d = pl.program_id(0)
    row = lax.broadcasted_iota(jnp.int32, (TILE, TILE), 0)
    col = lax.broadcasted_iota(jnp.int32, (TILE, TILE), 1)
    n = jnp.maximum(d * TILE + col - row, 0)
    max_exact = NUM_BUCKETS // 2
    n_large = jnp.maximum(n, max_exact).astype(F32)
    large = max_exact + (jnp.log(n_large / max_exact) / math.log(MAX_DISTANCE / max_exact)
                         * (NUM_BUCKETS - max_exact)).astype(jnp.int32)
    large = jnp.minimum(large, NUM_BUCKETS - 1)
    bucket = jnp.where(n < max_exact, n, large)
    for h in range(N_HEADS):
        acc = jnp.zeros((TILE, TILE), F32)
        for b in range(NUM_BUCKETS):
            acc = jnp.where(bucket == b, rb_ref[b, h], acc)
        o_ref[0, h] = acc


def _bias_tiles(rel_bias):
    return pl.pallas_call(
        _bias_tiles_kernel,
        grid=(N_BIAS_TILES,),
        in_specs=[pl.BlockSpec(memory_space=pltpu.SMEM)],
        out_specs=pl.BlockSpec((1, N_HEADS, TILE, TILE), lambda d: (d, 0, 0, 0)),
        out_shape=jax.ShapeDtypeStruct((N_BIAS_TILES, N_HEADS, TILE, TILE), F32),
        compiler_params=_params(1),
        name="t5_bias_tiles",
    )(rel_bias)


def _in_proj_kernel(x_ref, g_ref, w_ref, wvt_ref,
                    mla_ref, sbq_ref, sbk_ref, mbq_ref, mbqf_ref, mbk_ref, conv_ref, gate_ref,
                    sbvt_ref, mbvt_ref, kmean_ref):
    x = x_ref[...]
    ms = jnp.mean(x * x, axis=-1, keepdims=True)
    h = (x * lax.rsqrt(ms + EPS) * g_ref[...]).astype(BF16)

    def mm(lo, hi):
        return _dot(h, w_ref[:, lo:hi])

    c = 0
    mla_ref[...] = mm(c, c + MLA_U); c += MLA_U
    sbq_ref[...] = mm(c, c + GROUP).astype(BF16); c += GROUP
    sbk_ref[...] = mm(c, c + GROUP).astype(BF16); c += GROUP
    qf = mm(c, c + GROUP); c += GROUP
    mbqf_ref[...] = qf
    mbq_ref[...] = qf.astype(BF16)
    kf = mm(c, c + GROUP); c += GROUP
    mbk_ref[...] = kf.astype(BF16)
    conv_ref[...] = mm(c, c + 2 * GROUP); c += 2 * GROUP
    gate_ref[...] = mm(c, c + D_MIX)
    vt = _dot_nt(wvt_ref[...], h)
    for t in range(ROW_TILE // TILE):
        rows = slice(t * TILE, (t + 1) * TILE)
        kmean_ref[0, t] = jnp.mean(kf[rows], axis=0, keepdims=True)
        sbvt_ref[0, t] = vt[:GROUP, rows].astype(BF16)
        mbvt_ref[0, t] = vt[GROUP:, rows].astype(BF16)


def _in_proj(x2, g, w_main, w_vt, batch, seq):
    n = batch * seq
    nk = seq // TILE
    per_b = seq // ROW_TILE
    tpr = ROW_TILE // TILE
    row = lambda cols: pl.BlockSpec((ROW_TILE, cols), lambda i: (i, 0))
    full = lambda shp: pl.BlockSpec(shp, lambda i: (0,) * len(shp))
    vt_spec = pl.BlockSpec((1, tpr, GROUP, TILE), lambda i: (i // per_b, i % per_b, 0, 0))
    km_spec = pl.BlockSpec((1, tpr, 1, GROUP), lambda i: (i // per_b, i % per_b, 0, 0))
    sd = jax.ShapeDtypeStruct
    return pl.pallas_call(
        _in_proj_kernel,
        grid=(n // ROW_TILE,),
        in_specs=[row(D_MODEL), full((1, D_MODEL)), full((D_MODEL, MAIN_COLS)),
                  full((2 * GROUP, D_MODEL))],
        out_specs=[row(MLA_U), row(GROUP), row(GROUP), row(GROUP), row(GROUP), row(GROUP),
                   row(2 * GROUP), row(D_MIX), vt_spec, vt_spec, km_spec],
        out_shape=[sd((n, MLA_U), F32), sd((n, GROUP), BF16), sd((n, GROUP), BF16),
                   sd((n, GROUP), BF16), sd((n, GROUP), F32), sd((n, GROUP), BF16),
                   sd((n, 2 * GROUP), F32), sd((n, D_MIX), F32),
                   sd((batch, nk, GROUP, TILE), BF16), sd((batch, nk, GROUP, TILE), BF16),
                   sd((batch, nk, 1, GROUP), F32)],
        compiler_params=_params(1),
        name="in_proj",
    )(x2, g, w_main, w_vt)


def _mla_prep_kernel(u_ref, gq_ref, gkv_ref, wq_ref, wqs_ref, wkn_ref, wvt_ref,
                     cq_ref, sq_ref, ck_ref, sk_ref, q_ref, k_ref, vt_ref):
    u = u_ref[...]
    cq = u[:, 0:256]
    msq = jnp.sum(cq * cq, axis=-1, keepdims=True) * (1.0 / Q_RANK)
    cqn = (cq * lax.rsqrt(msq + EPS) * gq_ref[...]).astype(BF16)
    qa = _dot(cqn, wq_ref[...])
    qs = _dot(cqn, wqs_ref[...])
    ckv = u[:, 256:384]
    msk = jnp.mean(ckv * ckv, axis=-1, keepdims=True)
    ckvn = (ckv * lax.rsqrt(msk + EPS) * gkv_ref[...]).astype(BF16)
    kn = _dot(ckvn, wkn_ref[...])
    kr = u[:, 384:512] * ck_ref[...] + u[:, 512:640] * sk_ref[...]
    cq_t = cq_ref[...]
    sq_t = sq_ref[...]
    for h in range(N_HEADS):
        sl = slice(h * MLA_HEAD_PAD, (h + 1) * MLA_HEAD_PAD)
        q_ref[0, h] = (qa[:, sl] * cq_t + qs[:, sl] * sq_t).astype(BF16)
        k_ref[0, h] = (kn[:, sl] + kr).astype(BF16)
    vt = _dot_nt(wvt_ref[...], ckvn)
    for t in range(ROW_TILE // TILE):
        vt_ref[0, t] = vt[:, t * TILE:(t + 1) * TILE].astype(BF16)


def _mla_prep(mla_u, gq, gkv, wq, wqs, wkn, wvt, cq, sq, ck, sk, batch, seq):
    nk = seq // TILE
    per_b = seq // ROW_TILE
    tpr = ROW_TILE // TILE
    full = lambda shp: pl.BlockSpec(shp, lambda b, i: (0,) * len(shp))
    tab = pl.BlockSpec((ROW_TILE, MLA_HEAD_PAD), lambda b, i: (i, 0))
    head_spec = pl.BlockSpec((1, N_HEADS, ROW_TILE, MLA_HEAD_PAD), lambda b, i: (b, 0, i, 0))
    sd = jax.ShapeDtypeStruct
    return pl.pallas_call(
        _mla_prep_kernel,
        grid=(batch, per_b),
        in_specs=[pl.BlockSpec((ROW_TILE, MLA_U), lambda b, i: (b * per_b + i, 0)),
                  full((1, 256)), full((1, KV_RANK)),
                  full((256, N_HEADS * MLA_HEAD_PAD)), full((256, N_HEADS * MLA_HEAD_PAD)),
                  full((KV_RANK, N_HEADS * MLA_HEAD_PAD)), full((GROUP, KV_RANK)),
                  tab, tab, tab, tab],
        out_specs=[head_spec, head_spec,
                   pl.BlockSpec((1, tpr, GROUP, TILE), lambda b, i: (b, i, 0, 0))],
        out_shape=[sd((batch, N_HEADS, seq, MLA_HEAD_PAD), BF16),
                   sd((batch, N_HEADS, seq, MLA_HEAD_PAD), BF16),
                   sd((batch, nk, GROUP, TILE), BF16)],
        compiler_params=_params(2),
        name="mla_prep",
    )(mla_u, gq, gkv, wq, wqs, wkn, wvt, cq, sq, ck, sk)


def _head_rows(h):
    return slice(h * HEAD_DIM, (h + 1) * HEAD_DIM)


def _softmax_tile(s, vt, h, first, m_ref, l_ref, acc_ref):
    hs = _head_rows(h)
    m_tile = jnp.max(s, axis=0, keepdims=True)
    if first:
        p = jnp.exp(s - m_tile)
        m_ref[h] = m_tile
        l_ref[h] = jnp.sum(p, axis=0, keepdims=True)
        acc_ref[hs, :] = _dot(vt, p.astype(BF16))
    else:
        m_old = m_ref[h]
        m_new = jnp.maximum(m_old, m_tile)
        alpha = jnp.exp(m_old - m_new)
        p = jnp.exp(s - m_new)
        m_ref[h] = m_new
        l_ref[h] = alpha * l_ref[h] + jnp.sum(p, axis=0, keepdims=True)
        acc_ref[hs, :] = alpha * acc_ref[hs, :] + _dot(vt, p.astype(BF16))


def _softmax_finish(o_ref, l_ref, acc_ref):
    for h in range(N_HEADS):
        hs = _head_rows(h)
        acc_ref[hs, :] = acc_ref[hs, :] / l_ref[h]
    o_ref[0] = acc_ref[...].T


def _key_rows(j):
    return pl.ds(pl.multiple_of(j * TILE, TILE), TILE)


def _store_head_masked_q(q, qbd_ref):
    lane = lax.broadcasted_iota(jnp.int32, (TILE, GROUP), 1)
    for h in range(N_HEADS):
        in_head = (lane >= h * HEAD_DIM) & (lane < (h + 1) * HEAD_DIM)
        qbd_ref[h * TILE:(h + 1) * TILE, :] = jnp.where(in_head, q, jnp.zeros_like(q))


def _sweep_past_tiles(qi, tile):
    tile(qi, True)

    def body(j, carry):
        tile(j, False)
        return carry

    lax.fori_loop(0, qi, body, 0)


def _mla_attn_kernel(q_ref, k_ref, vt_ref, o_ref, m_ref, l_ref, acc_ref):
    qi = pl.program_id(1)
    row = lax.broadcasted_iota(jnp.int32, (TILE, TILE), 0)
    col = lax.broadcasted_iota(jnp.int32, (TILE, TILE), 1)

    def tile(j, first):
        scores = [_dot_nt(k_ref[0, h, _key_rows(j), :], q_ref[0, h]) for h in range(N_HEADS)]
        for h in range(N_HEADS):
            s = scores[h]
            if first:
                s = jnp.where(row <= col, s, -jnp.inf)
            _softmax_tile(s, vt_ref[0, j, _head_rows(h), :], h, first, m_ref, l_ref, acc_ref)

    _sweep_past_tiles(qi, tile)
    _softmax_finish(o_ref, l_ref, acc_ref)


def _attn_state_scratch():
    stat = pltpu.VMEM((N_HEADS, 1, TILE), F32)
    return [stat, stat, pltpu.VMEM((GROUP, TILE), F32)]


def _mla_attn(q, k, vt, batch, seq):
    nk = seq // TILE
    return pl.pallas_call(
        _mla_attn_kernel,
        grid=(batch, nk),
        in_specs=[pl.BlockSpec((1, N_HEADS, TILE, MLA_HEAD_PAD), lambda b, i: (b, 0, i, 0)),
                  pl.BlockSpec((1, N_HEADS, seq, MLA_HEAD_PAD), lambda b, i: (b, 0, 0, 0)),
                  pl.BlockSpec((1, nk, GROUP, TILE), lambda b, i: (b, 0, 0, 0))],
        out_specs=pl.BlockSpec((1, TILE, GROUP), lambda b, i: (b, i, 0)),
        out_shape=jax.ShapeDtypeStruct((batch, seq, GROUP), F32),
        scratch_shapes=_attn_state_scratch(),
        compiler_params=_params(2),
        name="mla_attn",
    )(q, k, vt)


def _split_bf16(x):
    hi = x.astype(BF16)
    lo = (x - hi.astype(F32)).astype(BF16)
    return hi, lo


def _sb_attn_kernel(q_ref, k_ref, vt_ref, o_ref, qbd_ref, carry_ref, acc_ref):
    qi = pl.program_id(1)
    _store_head_masked_q(q_ref[0], qbd_ref)
    row = lax.broadcasted_iota(jnp.int32, (TILE, TILE), 0)
    col = lax.broadcasted_iota(jnp.int32, (TILE, TILE), 1)
    past = row < col
    after = (col > row).astype(BF16)

    def tile(j, diag):
        z_all = _dot_nt(k_ref[0, _key_rows(j), :], qbd_ref[...])
        for h in range(N_HEADS):
            hs = _head_rows(h)
            z = z_all[:, h * TILE:(h + 1) * TILE]
            sp = jnp.log1p(jnp.exp(-jnp.abs(z)))
            log_beta = jnp.minimum(z, 0.0) - sp
            log_1m = log_beta - z
            if diag:
                log_1m = jnp.where(past, log_1m, 0.0)
            hi, lo = _split_bf16(log_1m)
            suffix = _dot(after, hi) + _dot(after, lo)
            tile_sum = suffix[0:1, :] + log_1m[0:1, :]
            if diag:
                a = jnp.where(past, jnp.exp(log_beta + suffix), 0.0)
                carry_ref[h] = tile_sum
                acc_ref[hs, :] = _dot(vt_ref[0, j, hs, :], a.astype(BF16))
            else:
                carry = carry_ref[h]
                a = jnp.exp(log_beta + suffix + carry)
                carry_ref[h] = carry + tile_sum
                acc_ref[hs, :] = acc_ref[hs, :] + _dot(vt_ref[0, j, hs, :], a.astype(BF16))

    def live():
        return jnp.max(carry_ref[...]) > SB_EXP_UNDERFLOW

    tile(qi, True)

    def body(c):
        tile(qi - 1 - c[0], False)
        return c[0] + 1, live()

    lax.while_loop(lambda c: (c[0] < qi) & c[1], body, (jnp.int32(0), live()))
    o_ref[0] = acc_ref[...].T


def _sb_attn(q, k, vt, batch, seq):
    nk = seq // TILE
    return pl.pallas_call(
        _sb_attn_kernel,
        grid=(batch, nk),
        in_specs=[pl.BlockSpec((1, TILE, GROUP), lambda b, i: (b, i, 0)),
                  pl.BlockSpec((1, seq, GROUP), lambda b, i: (b, 0, 0)),
                  pl.BlockSpec((1, nk, GROUP, TILE), lambda b, i: (b, 0, 0, 0))],
        out_specs=pl.BlockSpec((1, TILE, GROUP), lambda b, i: (b, i, 0)),
        out_shape=jax.ShapeDtypeStruct((batch, seq, GROUP), F32),
        scratch_shapes=[pltpu.VMEM((N_HEADS * TILE, GROUP), BF16), pltpu.VMEM((N_HEADS, 1, TILE), F32),
                        pltpu.VMEM((GROUP, TILE), F32)],
        compiler_params=_params(2),
        name="sb_attn",
    )(q, k, vt)


def _moba_attn_kernel(nk, q_ref, qf_ref, k_ref, vt_ref, km_ref, bias_ref, o_ref,
                      qbd_ref, sel_ref, m_ref, l_ref, acc_ref):
    qi = pl.program_id(1)
    _store_head_masked_q(q_ref[0], qbd_ref)
    qf = qf_ref[0]
    km_hi, km_lo = _split_bf16(km_ref[0, :, 0, :])
    row = lax.broadcasted_iota(jnp.int32, (TILE, TILE), 0)
    col = lax.broadcasted_iota(jnp.int32, (TILE, TILE), 1)
    blk = lax.broadcasted_iota(jnp.int32, (nk, TILE), 0)
    lane = lax.broadcasted_iota(jnp.int32, (TILE, GROUP), 1)
    for h in range(N_HEADS):
        in_head = (lane >= h * HEAD_DIM) & (lane < (h + 1) * HEAD_DIM)
        qf_hi, qf_lo = _split_bf16(jnp.where(in_head, qf, 0.0))
        gate = _dot_nt(km_hi, qf_hi) + (_dot_nt(km_hi, qf_lo) + _dot_nt(km_lo, qf_hi))
        beaten = jnp.zeros((nk, TILE), jnp.int32)
        for jp in range(nk):
            gj = gate[jp:jp + 1, :]
            beats = (gj > gate) | ((gj == gate) & (jp < blk))
            beaten = beaten + jnp.where(beats & (jp < qi), 1, 0)
        keep = (blk < qi) & (beaten < MOBA_TOPK)
        sel_ref[h] = jnp.where(keep, 0.0, -jnp.inf)

    def tile(j, first):
        s_all = _dot_nt(k_ref[0, _key_rows(j), :], qbd_ref[...])
        d = jnp.minimum(qi - j, N_BIAS_TILES - 1)
        for h in range(N_HEADS):
            s = s_all[:, h * TILE:(h + 1) * TILE] + bias_ref[d, h]
            if first:
                s = jnp.where(row <= col, s, -jnp.inf)
            else:
                s = s + sel_ref[h, pl.ds(j, 1), :]
            _softmax_tile(s, vt_ref[0, j, _head_rows(h), :], h, first, m_ref, l_ref, acc_ref)

    _sweep_past_tiles(qi, tile)
    _softmax_finish(o_ref, l_ref, acc_ref)


def _moba_attn(q, qf, k, vt, kmean, bias, batch, seq):
    nk = seq // TILE
    return pl.pallas_call(
        functools.partial(_moba_attn_kernel, nk),
        grid=(batch, nk),
        in_specs=[pl.BlockSpec((1, TILE, GROUP), lambda b, i: (b, i, 0)),
                  pl.BlockSpec((1, TILE, GROUP), lambda b, i: (b, i, 0)),
                  pl.BlockSpec((1, seq, GROUP), lambda b, i: (b, 0, 0)),
                  pl.BlockSpec((1, nk, GROUP, TILE), lambda b, i: (b, 0, 0, 0)),
                  pl.BlockSpec((1, nk, 1, GROUP), lambda b, i: (b, 0, 0, 0)),
                  pl.BlockSpec((N_BIAS_TILES, N_HEADS, TILE, TILE), lambda b, i: (0, 0, 0, 0))],
        out_specs=pl.BlockSpec((1, TILE, GROUP), lambda b, i: (b, i, 0)),
        out_shape=jax.ShapeDtypeStruct((batch, seq, GROUP), F32),
        scratch_shapes=[pltpu.VMEM((N_HEADS * TILE, GROUP), BF16), pltpu.VMEM((N_HEADS, nk, TILE), F32)]
        + _attn_state_scratch(),
        compiler_params=_params(2),
        name="moba_attn",
    )(q, qf, k, vt, kmean, bias)


def _conv_kernel(u_ref, halo_ref, dw_ref, dwb_ref, lng_ref, lnb_ref, pw_ref, pwb_ref, o_ref, ext_ref):
    i = pl.program_id(1)
    um = u_ref[0]
    uh = halo_ref[0]
    xh = uh[:, :GROUP] * jax.nn.sigmoid(uh[:, GROUP:])
    ext_ref[0:CONV_HALO, :] = jnp.where(i == 0, 0.0, xh)
    ext_ref[CONV_HALO:, :] = um[:, :GROUP] * jax.nn.sigmoid(um[:, GROUP:])
    base = CONV_HALO - (CONV_WIDTH - 1)
    y = jnp.zeros((ROW_TILE, GROUP), F32)
    for w in range(CONV_WIDTH):
        y = y + ext_ref[base + w:base + w + ROW_TILE, :] * dw_ref[w:w + 1, :]
    y = y + dwb_ref[...]
    mu = jnp.mean(y, axis=-1, keepdims=True)
    yc = y - mu
    var = jnp.mean(yc * yc, axis=-1, keepdims=True)
    yn = yc * lax.rsqrt(var + EPS) * lng_ref[...] + lnb_ref[...]
    sw = yn * jax.nn.sigmoid(yn)
    o_ref[0] = _dot(sw.astype(BF16), pw_ref[...]) + pwb_ref[...]


def _conv_module(u_conv, dw, dwb, lng, lnb, pw, pwb, batch, seq):
    per_b = seq // ROW_TILE
    halo_per_tile = ROW_TILE // CONV_HALO
    full = lambda shp: pl.BlockSpec(shp, lambda b, i: (0,) * len(shp))
    return pl.pallas_call(
        _conv_kernel,
        grid=(batch, per_b),
        in_specs=[pl.BlockSpec((1, ROW_TILE, 2 * GROUP), lambda b, i: (b, i, 0)),
                  pl.BlockSpec((1, CONV_HALO, 2 * GROUP),
                               lambda b, i: (b, jnp.maximum(i * halo_per_tile - 1, 0), 0)),
                  full((CONV_HALO, GROUP)), full((1, GROUP)), full((1, GROUP)), full((1, GROUP)),
                  full((GROUP, GROUP)), full((1, GROUP))],
        out_specs=pl.BlockSpec((1, ROW_TILE, GROUP), lambda b, i: (b, i, 0)),
        out_shape=jax.ShapeDtypeStruct((batch, seq, GROUP), F32),
        scratch_shapes=[pltpu.VMEM((CONV_HALO + ROW_TILE, GROUP), F32)],
        compiler_params=_params(2),
        name="conv_module",
    )(u_conv, u_conv, dw, dwb, lng, lnb, pw, pwb)


def _out_proj_kernel(oa_ref, ob_ref, oc_ref, od_ref, gate_ref, w_ref, g_ref, x_ref, o_ref):
    gt = gate_ref[...]
    sg = gt * jax.nn.sigmoid(gt)
    mix = jnp.concatenate([oa_ref[...], ob_ref[...], oc_ref[...], od_ref[...]], axis=-1)
    y = _dot((mix * sg).astype(BF16), w_ref[...])
    ms = jnp.mean(y * y, axis=-1, keepdims=True)
    o_ref[...] = x_ref[...] + y * lax.rsqrt(ms + EPS) * g_ref[...]


def _out_proj(oa, ob, oc, od, gate, w, g, x2):
    n = x2.shape[0]
    row = lambda cols: pl.BlockSpec((ROW_TILE, cols), lambda i: (i, 0))
    full = lambda shp: pl.BlockSpec(shp, lambda i: (0,) * len(shp))
    return pl.pallas_call(
        _out_proj_kernel,
        grid=(n // ROW_TILE,),
        in_specs=[row(GROUP), row(GROUP), row(GROUP), row(GROUP), row(D_MIX),
                  full((D_MIX, D_MODEL)), full((1, D_MODEL)), row(D_MODEL)],
        out_specs=row(D_MODEL),
        out_shape=jax.ShapeDtypeStruct((n, D_MODEL), F32),
        compiler_params=_params(1),
        name="out_proj",
    )(oa, ob, oc, od, gate, w, g, x2)


def _prep_in_proj_weights(w_in):
    depth = w_in.shape[0]
    z = lambda cols: jnp.zeros((depth, D_MODEL, cols), w_in.dtype)
    c = 0
    cq = w_in[..., c:c + Q_RANK]; c += Q_RANK
    ckv = w_in[..., c:c + KV_RANK]; c += KV_RANK
    kr = w_in[..., c:c + ROPE_DIM]; c += ROPE_DIM
    sbq, sbk, sbv = (w_in[..., c + i * GROUP:c + (i + 1) * GROUP] for i in range(3)); c += 3 * GROUP
    mbq, mbk, mbv = (w_in[..., c + i * GROUP:c + (i + 1) * GROUP] for i in range(3)); c += 3 * GROUP
    conv = w_in[..., c:c + 2 * GROUP]; c += 2 * GROUP
    gate = w_in[..., c:c + D_MIX]
    kr_swapped = jnp.concatenate([kr[..., ROPE_HALF:], kr[..., :ROPE_HALF]], axis=-1)
    pad_rope = lambda r: jnp.concatenate([z(HEAD_DIM), r, z(MLA_HEAD_PAD - HEAD_DIM - ROPE_DIM)], axis=-1)
    scale = HEAD_DIM ** -0.5
    main = jnp.concatenate([cq, z(256 - Q_RANK), ckv, pad_rope(kr), pad_rope(kr_swapped),
                            sbq * scale, sbk, mbq * scale, mbk, conv, gate], axis=-1)
    vt = jnp.concatenate([jnp.swapaxes(sbv, 1, 2), jnp.swapaxes(mbv, 1, 2)], axis=1)
    return main.astype(BF16), vt.astype(BF16)


def _prep_mla_weights(w_uq, w_ukv):
    depth = w_uq.shape[0]
    qh = w_uq.reshape(depth, Q_RANK, N_HEADS, HEAD_DIM + ROPE_DIM)
    nope, r1, r2 = qh[..., :HEAD_DIM], qh[..., HEAD_DIM:HEAD_DIM + ROPE_HALF], qh[..., HEAD_DIM + ROPE_HALF:]
    zpad = jnp.zeros((depth, Q_RANK, N_HEADS, MLA_HEAD_PAD - HEAD_DIM - ROPE_DIM), w_uq.dtype)
    row_pad = ((0, 0), (0, 256 - Q_RANK), (0, 0))
    wq = jnp.pad(jnp.concatenate([nope, r1, r2, zpad], -1).reshape(depth, Q_RANK, -1), row_pad)
    wqs = jnp.pad(jnp.concatenate([jnp.zeros_like(nope), r2, r1, zpad], -1).reshape(depth, Q_RANK, -1), row_pad)
    kvh = w_ukv.reshape(depth, KV_RANK, N_HEADS, 2 * HEAD_DIM)
    k_nope, v = kvh[..., :HEAD_DIM], kvh[..., HEAD_DIM:]
    wkn = jnp.concatenate([k_nope, jnp.zeros_like(k_nope)], -1).reshape(depth, KV_RANK, -1)
    wvt = jnp.swapaxes(v.reshape(depth, KV_RANK, GROUP), 1, 2)
    return wq.astype(BF16), wqs.astype(BF16), wkn.astype(BF16), wvt.astype(BF16)


def _rope_tables(seq):
    freqs = ROPE_THETA ** (-jnp.arange(ROPE_HALF, dtype=F32) / ROPE_HALF)
    ang = jnp.arange(seq, dtype=jnp.int32).astype(F32)[:, None] * freqs[None, :]
    cos, sin = jnp.cos(ang), jnp.sin(ang)
    ones, zeros = jnp.ones((seq, HEAD_DIM), F32), jnp.zeros((seq, HEAD_DIM), F32)
    ztail = jnp.zeros((seq, MLA_HEAD_PAD - HEAD_DIM - ROPE_DIM), F32)
    c_tab = jnp.concatenate([ones, cos, cos, ztail], axis=-1)
    s_tab = jnp.concatenate([zeros, -sin, sin, ztail], axis=-1)
    return c_tab, s_tab


def kernel(x, pre_norm_g, w_in, mla_q_norm_g, mla_w_uq, mla_kv_norm_g, mla_w_ukv, rel_bias, conv_dw_w, conv_dw_b, conv_ln_g, conv_ln_b, conv_pw_w, conv_pw_b, w_out, post_norm_g):
    batch, seq, d_model = x.shape
    depth = w_in.shape[0]
    assert d_model == D_MODEL and seq % ROW_TILE == 0 and ROW_TILE % TILE == 0
    n = batch * seq

    w_main, w_vt = _prep_in_proj_weights(w_in)
    wq, wqs, wkn, wvt_mla = _prep_mla_weights(mla_w_uq, mla_w_ukv)
    c_tab, s_tab = _rope_tables(seq)
    mla_scale = (HEAD_DIM + ROPE_DIM) ** -0.5
    cq_tab, sq_tab = c_tab * mla_scale, s_tab * mla_scale
    gq = jnp.pad(mla_q_norm_g, ((0, 0), (0, 256 - Q_RANK)))
    dw = jnp.pad(conv_dw_w, ((0, 0), (0, CONV_HALO - CONV_WIDTH), (0, 0)))
    pw = conv_pw_w.astype(BF16)
    w_o = w_out.astype(BF16)
    bias = _bias_tiles(rel_bias)

    x2 = x.reshape(n, D_MODEL)
    for l in range(depth):
        (mla_u, sbq, sbk, mbq, mbqf, mbk, u_conv, u_gate, sbvt, mbvt, kmean) = _in_proj(
            x2, pre_norm_g[l][None], w_main[l], w_vt[l], batch, seq)
        qcat, kcat, mla_vt = _mla_prep(mla_u, gq[l][None], mla_kv_norm_g[l][None], wq[l], wqs[l], wkn[l],
                                       wvt_mla[l], cq_tab, sq_tab, c_tab, s_tab, batch, seq)
        o_a = _mla_attn(qcat, kcat, mla_vt, batch, seq)
        o_b = _sb_attn(sbq.reshape(batch, seq, GROUP), sbk.reshape(batch, seq, GROUP), sbvt, batch, seq)
        o_c = _moba_attn(mbq.reshape(batch, seq, GROUP), mbqf.reshape(batch, seq, GROUP),
                         mbk.reshape(batch, seq, GROUP), mbvt, kmean, bias, batch, seq)
        o_d = _conv_module(u_conv.reshape(batch, seq, 2 * GROUP), dw[l], conv_dw_b[l][None],
                           conv_ln_g[l][None], conv_ln_b[l][None], pw[l], conv_pw_b[l][None], batch, seq)
        x2 = _out_proj(o_a.reshape(n, GROUP), o_b.reshape(n, GROUP), o_c.reshape(n, GROUP),
                       o_d.reshape(n, GROUP), u_gate, w_o[l], post_norm_g[l][None], x2)
    return x2.reshape(batch, seq, D_MODEL)
```

```python
import functools
import math

import jax
import jax.numpy as jnp
from jax import lax
from jax.experimental import pallas as pl
from jax.experimental.pallas import tpu as pltpu

F32 = jnp.float32
BF16 = jnp.bfloat16

D_MODEL = 1024
HEAD_DIM = 64
N_HEADS = 4
GROUP = N_HEADS * HEAD_DIM
D_MIX = 4 * GROUP
Q_RANK = 192
KV_RANK = 128
ROPE_DIM = 32
ROPE_HALF = ROPE_DIM // 2
ROPE_THETA = 10000.0
MLA_HEAD_PAD = 128
CONV_WIDTH = 31
NUM_BUCKETS = 32
MAX_DISTANCE = 1024
MOBA_TOPK = 3
EPS = 1e-6

LANES = 128
TILE = 256
ROW_TILE = 512
CONV_HALO = 32
N_BIAS_TILES = 6
MLA_U = 640
MAIN_COLS = MLA_U + 4 * GROUP + 2 * GROUP + D_MIX
VMEM_LIMIT = 56 * 1024 * 1024
SB_EXP_UNDERFLOW = -104.0
SCAN_RUN = TILE // 8
ONES_ROWS = 16
ACC_ROWS = HEAD_DIM + 8
LOG2E = math.log2(math.e)

_NT = (((1,), (1,)), ((), ()))


def _params(n_axes):
    return pltpu.CompilerParams(dimension_semantics=("arbitrary",) * n_axes,
                                vmem_limit_bytes=VMEM_LIMIT)


def _dot(a, b):
    return jnp.dot(a, b, preferred_element_type=F32)


def _dot_nt(a, b):
    return lax.dot_general(a, b, _NT, preferred_element_type=F32)


def _bias_tiles_kernel(rb_ref, o_ref):
    d = pl.program_id(0)
    row = lax.broadcasted_iota(jnp.int32, (TILE, TILE), 0)
    col = lax.broadcasted_iota(jnp.int32, (TILE, TILE), 1)
    n = jnp.maximum(d * TILE + col - row, 0)
    max_exact = NUM_BUCKETS // 2
    n_large = jnp.maximum(n, max_exact).astype(F32)
    large = max_exact + (jnp.log(n_large / max_exact) / math.log(MAX_DISTANCE / max_exact)
                         * (NUM_BUCKETS - max_exact)).astype(jnp.int32)
    large = jnp.minimum(large, NUM_BUCKETS - 1)
    bucket = jnp.where(n < max_exact, n, large)
    for h in range(N_HEADS):
        acc = jnp.zeros((TILE, TILE), F32)
        for b in range(NUM_BUCKETS):
            acc = jnp.where(bucket == b, rb_ref[b, h], acc)
        o_ref[0, h] = acc * LOG2E


def _bias_tiles(rel_bias):
    return pl.pallas_call(
        _bias_tiles_kernel,
        grid=(N_BIAS_TILES,),
        in_specs=[pl.BlockSpec(memory_space=pltpu.SMEM)],
        out_specs=pl.BlockSpec((1, N_HEADS, TILE, TILE), lambda d: (d, 0, 0, 0)),
        out_shape=jax.ShapeDtypeStruct((N_BIAS_TILES, N_HEADS, TILE, TILE), F32),
        compiler_params=_params(1),
        name="t5_bias_tiles",
    )(rel_bias)


def _in_proj_kernel(x_ref, g_ref, w_ref, wvt_ref,
                    mla_ref, sbq_ref, sbk_ref, mbq_ref, mbqf_ref, mbk_ref, conv_ref, gate_ref,
                    sbvt_ref, mbvt_ref, kmean_ref, hf_ref, hperm_ref):
    x = x_ref[...]
    ms = jnp.mean(x * x, axis=-1, keepdims=True)
    hf = x * lax.rsqrt(ms + EPS) * g_ref[...]
    h = hf.astype(BF16)
    for c in range(D_MODEL // LANES):
        hf_ref[c] = hf[:, c * LANES:(c + 1) * LANES]
    for c in range(D_MODEL // LANES):
        for t in range(ROW_TILE // TILE):
            for i in range(SCAN_RUN):
                hperm_ref[t * TILE + i * 8:t * TILE + (i + 1) * 8, c * LANES:(c + 1) * LANES] = (
                    hf_ref[c, pl.ds(t * TILE + i, 8, stride=SCAN_RUN), :])
    h_scan = hperm_ref[...].astype(BF16)

    def mm(lo, hi):
        return _dot(h, w_ref[:, lo:hi])

    c = 0
    mla_ref[...] = mm(c, c + MLA_U); c += MLA_U
    sbq_ref[...] = mm(c, c + GROUP).astype(BF16); c += GROUP
    sbk_ref[...] = _dot(h_scan, w_ref[:, c:c + GROUP]).astype(BF16); c += GROUP
    qf = mm(c, c + GROUP); c += GROUP
    mbqf_ref[...] = qf
    mbq_ref[...] = (qf * LOG2E).astype(BF16)
    kf = mm(c, c + GROUP); c += GROUP
    mbk_ref[...] = kf.astype(BF16)
    conv_ref[...] = mm(c, c + 2 * GROUP); c += 2 * GROUP
    gate_ref[...] = mm(c, c + D_MIX)
    sb_vt = _dot_nt(wvt_ref[:GROUP, :], h_scan)
    mb_vt = _dot_nt(wvt_ref[GROUP:, :], h)
    for t in range(ROW_TILE // TILE):
        rows = slice(t * TILE, (t + 1) * TILE)
        kmean_ref[0, t] = jnp.mean(kf[rows], axis=0, keepdims=True)
        sbvt_ref[0, t] = sb_vt[:, rows].astype(BF16)
        mbvt_ref[0, t] = mb_vt[:, rows].astype(BF16)


def _in_proj(x2, g, w_main, w_vt, batch, seq):
    n = batch * seq
    nk = seq // TILE
    per_b = seq // ROW_TILE
    tpr = ROW_TILE // TILE
    row = lambda cols: pl.BlockSpec((ROW_TILE, cols), lambda i: (i, 0))
    full = lambda shp: pl.BlockSpec(shp, lambda i: (0,) * len(shp))
    vt_spec = pl.BlockSpec((1, tpr, GROUP, TILE), lambda i: (i // per_b, i % per_b, 0, 0))
    km_spec = pl.BlockSpec((1, tpr, 1, GROUP), lambda i: (i // per_b, i % per_b, 0, 0))
    sd = jax.ShapeDtypeStruct
    return pl.pallas_call(
        _in_proj_kernel,
        grid=(n // ROW_TILE,),
        in_specs=[row(D_MODEL), full((1, D_MODEL)), full((D_MODEL, MAIN_COLS)),
                  full((2 * GROUP, D_MODEL))],
        out_specs=[row(MLA_U), row(GROUP), row(GROUP), row(GROUP), row(GROUP), row(GROUP),
                   row(2 * GROUP), row(D_MIX), vt_spec, vt_spec, km_spec],
        out_shape=[sd((n, MLA_U), F32), sd((n, GROUP), BF16), sd((n, GROUP), BF16),
                   sd((n, GROUP), BF16), sd((n, GROUP), F32), sd((n, GROUP), BF16),
                   sd((n, 2 * GROUP), F32), sd((n, D_MIX), F32),
                   sd((batch, nk, GROUP, TILE), BF16), sd((batch, nk, GROUP, TILE), BF16),
                   sd((batch, nk, 1, GROUP), F32)],
        scratch_shapes=[pltpu.VMEM((D_MODEL // LANES, ROW_TILE, LANES), F32),
                        pltpu.VMEM((ROW_TILE, D_MODEL), F32)],
        compiler_params=_params(1),
        name="in_proj",
    )(x2, g, w_main, w_vt)


def _mla_prep_kernel(u_ref, gq_ref, gkv_ref, wq_ref, wqs_ref, wkn_ref, wvt_ref,
                     cq_ref, sq_ref, ck_ref, sk_ref, q_ref, k_ref, vt_ref):
    u = u_ref[...]
    cq = u[:, 0:256]
    msq = jnp.sum(cq * cq, axis=-1, keepdims=True) * (1.0 / Q_RANK)
    cqn = (cq * lax.rsqrt(msq + EPS) * gq_ref[...]).astype(BF16)
    qa = _dot(cqn, wq_ref[...])
    qs = _dot(cqn, wqs_ref[...])
    ckv = u[:, 256:384]
    msk = jnp.mean(ckv * ckv, axis=-1, keepdims=True)
    ckvn = (ckv * lax.rsqrt(msk + EPS) * gkv_ref[...]).astype(BF16)
    kn = _dot(ckvn, wkn_ref[...])
    kr = u[:, 384:512] * ck_ref[...] + u[:, 512:640] * sk_ref[...]
    cq_t = cq_ref[...]
    sq_t = sq_ref[...]
    for h in range(N_HEADS):
        sl = slice(h * MLA_HEAD_PAD, (h + 1) * MLA_HEAD_PAD)
        q_ref[0, h] = (qa[:, sl] * cq_t + qs[:, sl] * sq_t).astype(BF16)
        k_ref[0, h] = (kn[:, sl] + kr).astype(BF16)
    vt = _dot_nt(wvt_ref[...], ckvn)
    for t in range(ROW_TILE // TILE):
        vt_ref[0, t] = vt[:, t * TILE:(t + 1) * TILE].astype(BF16)


def _mla_prep(mla_u, gq, gkv, wq, wqs, wkn, wvt, cq, sq, ck, sk, batch, seq):
    nk = seq // TILE
    per_b = seq // ROW_TILE
    tpr = ROW_TILE // TILE
    full = lambda shp: pl.BlockSpec(shp, lambda b, i: (0,) * len(shp))
    tab = pl.BlockSpec((ROW_TILE, MLA_HEAD_PAD), lambda b, i: (i, 0))
    head_spec = pl.BlockSpec((1, N_HEADS, ROW_TILE, MLA_HEAD_PAD), lambda b, i: (b, 0, i, 0))
    sd = jax.ShapeDtypeStruct
    return pl.pallas_call(
        _mla_prep_kernel,
        grid=(batch, per_b),
        in_specs=[pl.BlockSpec((ROW_TILE, MLA_U), lambda b, i: (b * per_b + i, 0)),
                  full((1, 256)), full((1, KV_RANK)),
                  full((256, N_HEADS * MLA_HEAD_PAD)), full((256, N_HEADS * MLA_HEAD_PAD)),
                  full((KV_RANK, N_HEADS * MLA_HEAD_PAD)), full((GROUP, KV_RANK)),
                  tab, tab, tab, tab],
        out_specs=[head_spec, head_spec,
                   pl.BlockSpec((1, tpr, GROUP, TILE), lambda b, i: (b, i, 0, 0))],
        out_shape=[sd((batch, N_HEADS, seq, MLA_HEAD_PAD), BF16),
                   sd((batch, N_HEADS, seq, MLA_HEAD_PAD), BF16),
                   sd((batch, nk, GROUP, TILE), BF16)],
        compiler_params=_params(2),
        name="mla_prep",
    )(mla_u, gq, gkv, wq, wqs, wkn, wvt, cq, sq, ck, sk)


def _head_rows(h):
    return slice(h * HEAD_DIM, (h + 1) * HEAD_DIM)


def _head_cols(h):
    return slice(h * TILE, (h + 1) * TILE)


def _key_rows(j):
    return pl.ds(pl.multiple_of(j * TILE, TILE), TILE)


def _store_head_masked_q(q, qbd_ref):
    lane = lax.broadcasted_iota(jnp.int32, (TILE, GROUP), 1)
    for h in range(N_HEADS):
        in_head = (lane >= h * HEAD_DIM) & (lane < (h + 1) * HEAD_DIM)
        qbd_ref[h * TILE:(h + 1) * TILE, :] = jnp.where(in_head, q, jnp.zeros_like(q))


def _sweep_scratch():
    s_buf = pltpu.VMEM((TILE, N_HEADS * TILE), F32)
    p_buf = pltpu.VMEM((N_HEADS, TILE, TILE), BF16)
    stat = pltpu.VMEM((N_HEADS, 1, TILE), F32)
    return [s_buf, s_buf, p_buf, p_buf, stat, stat, stat, pltpu.VMEM((N_HEADS, ACC_ROWS, TILE), F32)]


def _softmax_sweep(qi, score_fn, adjust_fn, vt_ref, o_ref, scratch):
    s_a, s_b, p_a, p_b, al_a, al_b, m_ref, acc_ref = scratch
    row = lax.broadcasted_iota(jnp.int32, (TILE, TILE), 0)
    col = lax.broadcasted_iota(jnp.int32, (TILE, TILE), 1)
    ones_rows = jnp.ones((ONES_ROWS, TILE), BF16)

    def key_tile(t):
        if isinstance(t, int) and t == 0:
            return qi
        return jnp.minimum(t - 1, jnp.maximum(qi - 1, 0))

    def softmax(h, t, s_ref, p_ref, al_ref):
        first = isinstance(t, int) and t == 0
        s, pen = adjust_fn(h, key_tile(t), s_ref[:, _head_cols(h)], first)
        if first:
            s = jnp.where(row <= col, s, -jnp.inf)
            m_new = jnp.max(s, axis=0, keepdims=True)
            p = jnp.exp2(s - m_new)
        else:
            dead = jnp.where(t <= qi, 0.0, -jnp.inf)
            pen = dead if pen is None else pen + dead
            m_old = m_ref[h]
            m_new = jnp.maximum(m_old, jnp.max(s, axis=0, keepdims=True) + pen)
            alpha = jnp.exp2(m_old - m_new)
            p = jnp.exp2(s - (m_new - pen))
            al_ref[h] = alpha
        m_ref[h] = m_new
        p_ref[h] = p.astype(BF16)

    def weighted_values(h, t, p_ref, al_ref):
        first = isinstance(t, int) and t == 0
        vt = jnp.concatenate([vt_ref[0, key_tile(t), _head_rows(h), :], ones_rows], axis=0)
        pv = _dot(vt, p_ref[h])[:ACC_ROWS]
        acc_ref[h] = pv if first else acc_ref[h] * al_ref[h] + pv

    def half(t, cur, nxt):
        (s_cur, p_cur, al_cur), (s_nxt, p_nxt, al_nxt) = cur, nxt
        for h in range(N_HEADS):
            score_fn(h, key_tile(t + 1), s_nxt)
            if not (isinstance(t, int) and t == 0):
                weighted_values(h, t - 1, p_nxt, al_nxt)
            softmax(h, t, s_cur, p_cur, al_cur)

    buf_a, buf_b = (s_a, p_a, al_a), (s_b, p_b, al_b)
    for h in range(N_HEADS):
        score_fn(h, key_tile(0), s_a)
    half(0, buf_a, buf_b)
    half(1, buf_b, buf_a)

    def pair(k, carry):
        half(2 * k, buf_a, buf_b)
        half(2 * k + 1, buf_b, buf_a)
        return carry

    n_pairs = qi // 2 + 1
    lax.fori_loop(1, n_pairs, pair, 0)
    for h in range(N_HEADS):
        weighted_values(h, 2 * n_pairs - 1, p_b, al_b)
    out_t = [acc_ref[h, :HEAD_DIM, :] / acc_ref[h, HEAD_DIM:HEAD_DIM + 1, :] for h in range(N_HEADS)]
    o_ref[0] = jnp.concatenate(out_t, axis=0).T.astype(o_ref.dtype)


def _mla_attn_kernel(q_ref, k_ref, vt_ref, o_ref, *scratch):
    qi = pl.program_id(1)

    def score_fn(h, j, s_ref):
        s_ref[:, _head_cols(h)] = _dot_nt(k_ref[0, h, _key_rows(j), :], q_ref[0, h])

    _softmax_sweep(qi, score_fn, lambda h, j, s, first: (s, None), vt_ref, o_ref, scratch)


def _mla_attn(q, k, vt, batch, seq):
    nk = seq // TILE
    return pl.pallas_call(
        _mla_attn_kernel,
        grid=(batch, nk),
        in_specs=[pl.BlockSpec((1, N_HEADS, TILE, MLA_HEAD_PAD), lambda b, i: (b, 0, i, 0)),
                  pl.BlockSpec((1, N_HEADS, seq, MLA_HEAD_PAD), lambda b, i: (b, 0, 0, 0)),
                  pl.BlockSpec((1, nk, GROUP, TILE), lambda b, i: (b, 0, 0, 0))],
        out_specs=pl.BlockSpec((1, TILE, GROUP), lambda b, i: (b, i, 0)),
        out_shape=jax.ShapeDtypeStruct((batch, seq, GROUP), BF16),
        scratch_shapes=_sweep_scratch(),
        compiler_params=_params(2),
        name="mla_attn",
    )(q, k, vt)


def _split_bf16(x):
    hi = x.astype(BF16)
    lo = (x - hi.astype(F32)).astype(BF16)
    return hi, lo


def _suffix_scan(x, carry):
    slabs = [x[i * 8:(i + 1) * 8, :] for i in range(SCAN_RUN)]
    later_in_run = [None] * SCAN_RUN
    later_in_run[SCAN_RUN - 1] = jnp.zeros_like(slabs[0])
    for i in range(SCAN_RUN - 2, -1, -1):
        later_in_run[i] = later_in_run[i + 1] + slabs[i + 1]
    run_total = later_in_run[0] + slabs[0]
    sub = lax.broadcasted_iota(jnp.int32, run_total.shape, 0)
    from_here = run_total
    for d in (1, 2, 4):
        from_here = from_here + jnp.where(sub + d < 8, pltpu.roll(from_here, 8 - d, 0), 0.0)
    later_runs = from_here - run_total + carry
    suffix = jnp.concatenate([later_in_run[i] + later_runs for i in range(SCAN_RUN)], axis=0)
    return suffix, from_here[0:1, :]


def _sb_attn_kernel(q_ref, k_ref, vt_ref, o_ref, qbd_ref, carry_ref, acc_ref):
    qi = pl.program_id(1)
    _store_head_masked_q(q_ref[0], qbd_ref)
    row = lax.broadcasted_iota(jnp.int32, (TILE, TILE), 0)
    col = lax.broadcasted_iota(jnp.int32, (TILE, TILE), 1)
    token = (row & 7) * SCAN_RUN + (row >> 3)
    past = token < col

    def tile(j, diag):
        z_all = _dot_nt(k_ref[0, _key_rows(j), :], qbd_ref[...])
        for h in range(N_HEADS):
            hs = _head_rows(h)
            z = z_all[:, h * TILE:(h + 1) * TILE]
            sp = jnp.log1p(jnp.exp(-jnp.abs(z)))
            log_beta = jnp.minimum(z, 0.0) - sp
            log_1m = log_beta - z
            if diag:
                log_1m = jnp.where(past, log_1m, 0.0)
                suffix, tile_sum = _suffix_scan(log_1m, jnp.zeros((1, TILE), F32))
                a = jnp.where(past, jnp.exp(log_beta + suffix), 0.0)
                carry_ref[h] = tile_sum
                acc_ref[hs, :] = _dot(vt_ref[0, j, hs, :], a.astype(BF16))
            else:
                carry = carry_ref[h]
                suffix, tile_sum = _suffix_scan(log_1m, carry)
                a = jnp.exp(log_beta + suffix)
                carry_ref[h] = carry + tile_sum
                acc_ref[hs, :] = acc_ref[hs, :] + _dot(vt_ref[0, j, hs, :], a.astype(BF16))

    def live():
        return jnp.max(carry_ref[...]) > SB_EXP_UNDERFLOW

    tile(qi, True)

    def body(c):
        tile(qi - 1 - c[0], False)
        return c[0] + 1, live()

    lax.while_loop(lambda c: (c[0] < qi) & c[1], body, (jnp.int32(0), live()))
    o_ref[0] = acc_ref[...].T.astype(o_ref.dtype)


def _sb_attn(q, k, vt, batch, seq):
    nk = seq // TILE
    return pl.pallas_call(
        _sb_attn_kernel,
        grid=(batch, nk),
        in_specs=[pl.BlockSpec((1, TILE, GROUP), lambda b, i: (b, i, 0)),
                  pl.BlockSpec((1, seq, GROUP), lambda b, i: (b, 0, 0)),
                  pl.BlockSpec((1, nk, GROUP, TILE), lambda b, i: (b, 0, 0, 0))],
        out_specs=pl.BlockSpec((1, TILE, GROUP), lambda b, i: (b, i, 0)),
        out_shape=jax.ShapeDtypeStruct((batch, seq, GROUP), BF16),
        scratch_shapes=[pltpu.VMEM((N_HEADS * TILE, GROUP), BF16), pltpu.VMEM((N_HEADS, 1, TILE), F32),
                        pltpu.VMEM((GROUP, TILE), F32)],
        compiler_params=_params(2),
        name="sb_attn",
    )(q, k, vt)


def _moba_attn_kernel(nk, q_ref, qf_ref, k_ref, vt_ref, km_ref, bias_ref, o_ref,
                      qbd_ref, sel_ref, *scratch):
    qi = pl.program_id(1)
    _store_head_masked_q(q_ref[0], qbd_ref)
    qf = qf_ref[0]
    km_hi, km_lo = _split_bf16(km_ref[0, :, 0, :])
    blk = lax.broadcasted_iota(jnp.int32, (nk, TILE), 0)
    lane = lax.broadcasted_iota(jnp.int32, (TILE, GROUP), 1)
    for h in range(N_HEADS):
        in_head = (lane >= h * HEAD_DIM) & (lane < (h + 1) * HEAD_DIM)
        qf_hi, qf_lo = _split_bf16(jnp.where(in_head, qf, 0.0))
        gate = _dot_nt(km_hi, qf_hi) + (_dot_nt(km_hi, qf_lo) + _dot_nt(km_lo, qf_hi))
        beaten = jnp.zeros((nk, TILE), jnp.int32)
        for jp in range(nk):
            gj = gate[jp:jp + 1, :]
            beats = (gj > gate) | ((gj == gate) & (jp < blk))
            beaten = beaten + jnp.where(beats & (jp < qi), 1, 0)
        keep = (blk < qi) & (beaten < MOBA_TOPK)
        sel_ref[h] = jnp.where(keep, 0.0, -jnp.inf)

    def score_fn(h, j, s_ref):
        s_ref[:, _head_cols(h)] = _dot_nt(k_ref[0, _key_rows(j), :], qbd_ref[_head_cols(h), :])

    def adjust_fn(h, j, s, first):
        if first:
            return s + bias_ref[0, h], None
        d = jnp.minimum(qi - j, N_BIAS_TILES - 1)
        return s + bias_ref[d, h], sel_ref[h, pl.ds(j, 1), :]

    _softmax_sweep(qi, score_fn, adjust_fn, vt_ref, o_ref, scratch)


def _moba_attn(q, qf, k, vt, kmean, bias, batch, seq):
    nk = seq // TILE
    return pl.pallas_call(
        functools.partial(_moba_attn_kernel, nk),
        grid=(batch, nk),
        in_specs=[pl.BlockSpec((1, TILE, GROUP), lambda b, i: (b, i, 0)),
                  pl.BlockSpec((1, TILE, GROUP), lambda b, i: (b, i, 0)),
                  pl.BlockSpec((1, seq, GROUP), lambda b, i: (b, 0, 0)),
                  pl.BlockSpec((1, nk, GROUP, TILE), lambda b, i: (b, 0, 0, 0)),
                  pl.BlockSpec((1, nk, 1, GROUP), lambda b, i: (b, 0, 0, 0)),
                  pl.BlockSpec((N_BIAS_TILES, N_HEADS, TILE, TILE), lambda b, i: (0, 0, 0, 0))],
        out_specs=pl.BlockSpec((1, TILE, GROUP), lambda b, i: (b, i, 0)),
        out_shape=jax.ShapeDtypeStruct((batch, seq, GROUP), BF16),
        scratch_shapes=[pltpu.VMEM((N_HEADS * TILE, GROUP), BF16), pltpu.VMEM((N_HEADS, nk, TILE), F32)]
        + _sweep_scratch(),
        compiler_params=_params(2),
        name="moba_attn",
    )(q, qf, k, vt, kmean, bias)


def _conv_kernel(u_ref, halo_ref, dw_ref, dwb_ref, lng_ref, lnb_ref, pw_ref, pwb_ref, o_ref, ext_ref):
    i = pl.program_id(1)
    um = u_ref[0]
    uh = halo_ref[0]
    xh = uh[:, :GROUP] * jax.nn.sigmoid(uh[:, GROUP:])
    ext_ref[0:CONV_HALO, :] = jnp.where(i == 0, 0.0, xh)
    ext_ref[CONV_HALO:, :] = um[:, :GROUP] * jax.nn.sigmoid(um[:, GROUP:])
    base = CONV_HALO - (CONV_WIDTH - 1)
    y = jnp.zeros((ROW_TILE, GROUP), F32)
    for w in range(CONV_WIDTH):
        y = y + ext_ref[base + w:base + w + ROW_TILE, :] * dw_ref[w:w + 1, :]
    y = y + dwb_ref[...]
    mu = jnp.mean(y, axis=-1, keepdims=True)
    yc = y - mu
    var = jnp.mean(yc * yc, axis=-1, keepdims=True)
    yn = yc * lax.rsqrt(var + EPS) * lng_ref[...] + lnb_ref[...]
    sw = yn * jax.nn.sigmoid(yn)
    o_ref[0] = (_dot(sw.astype(BF16), pw_ref[...]) + pwb_ref[...]).astype(o_ref.dtype)


def _conv_module(u_conv, dw, dwb, lng, lnb, pw, pwb, batch, seq):
    per_b = seq // ROW_TILE
    halo_per_tile = ROW_TILE // CONV_HALO
    full = lambda shp: pl.BlockSpec(shp, lambda b, i: (0,) * len(shp))
    return pl.pallas_call(
        _conv_kernel,
        grid=(batch, per_b),
        in_specs=[pl.BlockSpec((1, ROW_TILE, 2 * GROUP), lambda b, i: (b, i, 0)),
                  pl.BlockSpec((1, CONV_HALO, 2 * GROUP),
                               lambda b, i: (b, jnp.maximum(i * halo_per_tile - 1, 0), 0)),
                  full((CONV_HALO, GROUP)), full((1, GROUP)), full((1, GROUP)), full((1, GROUP)),
                  full((GROUP, GROUP)), full((1, GROUP))],
        out_specs=pl.BlockSpec((1, ROW_TILE, GROUP), lambda b, i: (b, i, 0)),
        out_shape=jax.ShapeDtypeStruct((batch, seq, GROUP), BF16),
        scratch_shapes=[pltpu.VMEM((CONV_HALO + ROW_TILE, GROUP), F32)],
        compiler_params=_params(2),
        name="conv_module",
    )(u_conv, u_conv, dw, dwb, lng, lnb, pw, pwb)


def _out_proj_kernel(oa_ref, ob_ref, oc_ref, od_ref, gate_ref, w_ref, g_ref, x_ref, o_ref):
    gt = gate_ref[...]
    sg = gt * jax.nn.sigmoid(gt)
    mix = jnp.concatenate([oa_ref[...], ob_ref[...], oc_ref[...], od_ref[...]], axis=-1).astype(F32)
    y = _dot((mix * sg).astype(BF16), w_ref[...])
    ms = jnp.mean(y * y, axis=-1, keepdims=True)
    o_ref[...] = x_ref[...] + y * lax.rsqrt(ms + EPS) * g_ref[...]


def _out_proj(oa, ob, oc, od, gate, w, g, x2):
    n = x2.shape[0]
    row = lambda cols: pl.BlockSpec((ROW_TILE, cols), lambda i: (i, 0))
    full = lambda shp: pl.BlockSpec(shp, lambda i: (0,) * len(shp))
    return pl.pallas_call(
        _out_proj_kernel,
        grid=(n // ROW_TILE,),
        in_specs=[row(GROUP), row(GROUP), row(GROUP), row(GROUP), row(D_MIX),
                  full((D_MIX, D_MODEL)), full((1, D_MODEL)), row(D_MODEL)],
        out_specs=row(D_MODEL),
        out_shape=jax.ShapeDtypeStruct((n, D_MODEL), F32),
        compiler_params=_params(1),
        name="out_proj",
    )(oa, ob, oc, od, gate, w, g, x2)


def _prep_in_proj_weights(w_in):
    depth = w_in.shape[0]
    z = lambda cols: jnp.zeros((depth, D_MODEL, cols), w_in.dtype)
    c = 0
    cq = w_in[..., c:c + Q_RANK]; c += Q_RANK
    ckv = w_in[..., c:c + KV_RANK]; c += KV_RANK
    kr = w_in[..., c:c + ROPE_DIM]; c += ROPE_DIM
    sbq, sbk, sbv = (w_in[..., c + i * GROUP:c + (i + 1) * GROUP] for i in range(3)); c += 3 * GROUP
    mbq, mbk, mbv = (w_in[..., c + i * GROUP:c + (i + 1) * GROUP] for i in range(3)); c += 3 * GROUP
    conv = w_in[..., c:c + 2 * GROUP]; c += 2 * GROUP
    gate = w_in[..., c:c + D_MIX]
    kr_swapped = jnp.concatenate([kr[..., ROPE_HALF:], kr[..., :ROPE_HALF]], axis=-1)
    pad_rope = lambda r: jnp.concatenate([z(HEAD_DIM), r, z(MLA_HEAD_PAD - HEAD_DIM - ROPE_DIM)], axis=-1)
    scale = HEAD_DIM ** -0.5
    main = jnp.concatenate([cq, z(256 - Q_RANK), ckv, pad_rope(kr), pad_rope(kr_swapped),
                            sbq * scale, sbk, mbq * scale, mbk, conv, gate], axis=-1)
    vt = jnp.concatenate([jnp.swapaxes(sbv, 1, 2), jnp.swapaxes(mbv, 1, 2)], axis=1)
    return main.astype(BF16), vt.astype(BF16)


def _prep_mla_weights(w_uq, w_ukv):
    depth = w_uq.shape[0]
    qh = w_uq.reshape(depth, Q_RANK, N_HEADS, HEAD_DIM + ROPE_DIM)
    nope, r1, r2 = qh[..., :HEAD_DIM], qh[..., HEAD_DIM:HEAD_DIM + ROPE_HALF], qh[..., HEAD_DIM + ROPE_HALF:]
    zpad = jnp.zeros((depth, Q_RANK, N_HEADS, MLA_HEAD_PAD - HEAD_DIM - ROPE_DIM), w_uq.dtype)
    row_pad = ((0, 0), (0, 256 - Q_RANK), (0, 0))
    wq = jnp.pad(jnp.concatenate([nope, r1, r2, zpad], -1).reshape(depth, Q_RANK, -1), row_pad)
    wqs = jnp.pad(jnp.concatenate([jnp.zeros_like(nope), r2, r1, zpad], -1).reshape(depth, Q_RANK, -1), row_pad)
    kvh = w_ukv.reshape(depth, KV_RANK, N_HEADS, 2 * HEAD_DIM)
    k_nope, v = kvh[..., :HEAD_DIM], kvh[..., HEAD_DIM:]
    wkn = jnp.concatenate([k_nope, jnp.zeros_like(k_nope)], -1).reshape(depth, KV_RANK, -1)
    wvt = jnp.swapaxes(v.reshape(depth, KV_RANK, GROUP), 1, 2)
    return wq.astype(BF16), wqs.astype(BF16), wkn.astype(BF16), wvt.astype(BF16)


def _rope_tables(seq):
    freqs = ROPE_THETA ** (-jnp.arange(ROPE_HALF, dtype=F32) / ROPE_HALF)
    ang = jnp.arange(seq, dtype=jnp.int32).astype(F32)[:, None] * freqs[None, :]
    cos, sin = jnp.cos(ang), jnp.sin(ang)
    ones, zeros = jnp.ones((seq, HEAD_DIM), F32), jnp.zeros((seq, HEAD_DIM), F32)
    ztail = jnp.zeros((seq, MLA_HEAD_PAD - HEAD_DIM - ROPE_DIM), F32)
    c_tab = jnp.concatenate([ones, cos, cos, ztail], axis=-1)
    s_tab = jnp.concatenate([zeros, -sin, sin, ztail], axis=-1)
    return c_tab, s_tab


def kernel(x, pre_norm_g, w_in, mla_q_norm_g, mla_w_uq, mla_kv_norm_g, mla_w_ukv, rel_bias, conv_dw_w, conv_dw_b, conv_ln_g, conv_ln_b, conv_pw_w, conv_pw_b, w_out, post_norm_g):
    batch, seq, d_model = x.shape
    depth = w_in.shape[0]
    assert d_model == D_MODEL and seq % ROW_TILE == 0 and ROW_TILE % TILE == 0
    n = batch * seq

    w_main, w_vt = _prep_in_proj_weights(w_in)
    wq, wqs, wkn, wvt_mla = _prep_mla_weights(mla_w_uq, mla_w_ukv)
    c_tab, s_tab = _rope_tables(seq)
    mla_scale = (HEAD_DIM + ROPE_DIM) ** -0.5 * LOG2E
    cq_tab, sq_tab = c_tab * mla_scale, s_tab * mla_scale
    gq = jnp.pad(mla_q_norm_g, ((0, 0), (0, 256 - Q_RANK)))
    dw = jnp.pad(conv_dw_w, ((0, 0), (0, CONV_HALO - CONV_WIDTH), (0, 0)))
    pw = conv_pw_w.astype(BF16)
    w_o = w_out.astype(BF16)
    bias = _bias_tiles(rel_bias)

    x2 = x.reshape(n, D_MODEL)
    for l in range(depth):
        (mla_u, sbq, sbk, mbq, mbqf, mbk, u_conv, u_gate, sbvt, mbvt, kmean) = _in_proj(
            x2, pre_norm_g[l][None], w_main[l], w_vt[l], batch, seq)
        qcat, kcat, mla_vt = _mla_prep(mla_u, gq[l][None], mla_kv_norm_g[l][None], wq[l], wqs[l], wkn[l],
                                       wvt_mla[l], cq_tab, sq_tab, c_tab, s_tab, batch, seq)
        o_a = _mla_attn(qcat, kcat, mla_vt, batch, seq)
        o_b = _sb_attn(sbq.reshape(batch, seq, GROUP), sbk.reshape(batch, seq, GROUP), sbvt, batch, seq)
        o_c = _moba_attn(mbq.reshape(batch, seq, GROUP), mbqf.reshape(batch, seq, GROUP),
                         mbk.reshape(batch, seq, GROUP), mbvt, kmean, bias, batch, seq)
        o_d = _conv_module(u_conv.reshape(batch, seq, 2 * GROUP), dw[l], conv_dw_b[l][None],
                           conv_ln_g[l][None], conv_ln_b[l][None], pw[l], conv_pw_b[l][None], batch, seq)
        x2 = _out_proj(o_a.reshape(n, GROUP), o_b.reshape(n, GROUP), o_c.reshape(n, GROUP),
                       o_d.reshape(n, GROUP), u_gate, w_o[l], post_norm_g[l][None], x2)
    return x2.reshape(batch, seq, D_MODEL)
```

```python
import functools
import math

import jax
import jax.numpy as jnp
from jax import lax
from jax.experimental import pallas as pl
from jax.experimental.pallas import tpu as pltpu

F32 = jnp.float32
BF16 = jnp.bfloat16

D_MODEL = 1024
HEAD_DIM = 64
N_HEADS = 4
GROUP = N_HEADS * HEAD_DIM
D_MIX = 4 * GROUP
Q_RANK = 192
KV_RANK = 128
ROPE_DIM = 32
ROPE_HALF = ROPE_DIM // 2
ROPE_THETA = 10000.0
MLA_HEAD_PAD = 128
CONV_WIDTH = 31
NUM_BUCKETS = 32
MAX_DISTANCE = 1024
MOBA_TOPK = 3
EPS = 1e-6

LANES = 128
TILE = 256
ROW_TILE = 512
CONV_HALO = 32
CONV_SHIFT_ROWS = CONV_HALO + ROW_TILE - 8
N_BIAS_TILES = 6
MLA_U = 384
MAIN_COLS = MLA_U + 4 * GROUP + 2 * GROUP + D_MIX
VMEM_LIMIT = 56 * 1024 * 1024
SB_EXP2_UNDERFLOW = -150.0
SIGN_BIT = 0x80000000
SCAN_RUN = TILE // 8
ONES_ROWS = 16
ACC_ROWS = HEAD_DIM + 8
LOG2E = math.log2(math.e)

_NT = (((1,), (1,)), ((), ()))


def _params(n_axes):
    return pltpu.CompilerParams(dimension_semantics=("arbitrary",) * n_axes,
                                vmem_limit_bytes=VMEM_LIMIT)


def _dot(a, b):
    return jnp.dot(a, b, preferred_element_type=F32)


def _dot_nt(a, b):
    return lax.dot_general(a, b, _NT, preferred_element_type=F32)


def _bias_tiles_kernel(rb_ref, o_ref):
    d = pl.program_id(0)
    row = lax.broadcasted_iota(jnp.int32, (TILE, TILE), 0)
    col = lax.broadcasted_iota(jnp.int32, (TILE, TILE), 1)
    n = jnp.maximum(d * TILE + col - row, 0)
    max_exact = NUM_BUCKETS // 2
    n_large = jnp.maximum(n, max_exact).astype(F32)
    large = max_exact + (jnp.log(n_large / max_exact) / math.log(MAX_DISTANCE / max_exact)
                         * (NUM_BUCKETS - max_exact)).astype(jnp.int32)
    large = jnp.minimum(large, NUM_BUCKETS - 1)
    bucket = jnp.where(n < max_exact, n, large)
    for h in range(N_HEADS):
        acc = jnp.zeros((TILE, TILE), F32)
        for b in range(NUM_BUCKETS):
            acc = jnp.where(bucket == b, rb_ref[b, h], acc)
        o_ref[0, h] = acc * LOG2E


def _bias_tiles(rel_bias):
    return pl.pallas_call(
        _bias_tiles_kernel,
        grid=(N_BIAS_TILES,),
        in_specs=[pl.BlockSpec(memory_space=pltpu.SMEM)],
        out_specs=pl.BlockSpec((1, N_HEADS, TILE, TILE), lambda d: (d, 0, 0, 0)),
        out_shape=jax.ShapeDtypeStruct((N_BIAS_TILES, N_HEADS, TILE, TILE), F32),
        compiler_params=_params(1),
        name="t5_bias_tiles",
    )(rel_bias)


def _in_proj_kernel(x_ref, g_ref, w_ref, wvt_ref,
                    mla_ref, sbq_ref, sbk_ref, mbq_ref, mbqf_ref, mbk_ref, conv_ref, gate_ref,
                    sbvt_ref, mbvt_ref, kmean_ref, hf_ref, hperm_ref):
    x = x_ref[...]
    ms = jnp.mean(x * x, axis=-1, keepdims=True)
    hf = x * lax.rsqrt(ms + EPS) * g_ref[...]
    h = hf.astype(BF16)
    for c in range(D_MODEL // LANES):
        hf_ref[c] = hf[:, c * LANES:(c + 1) * LANES]
    for c in range(D_MODEL // LANES):
        for t in range(ROW_TILE // TILE):
            for i in range(SCAN_RUN):
                hperm_ref[t * TILE + i * 8:t * TILE + (i + 1) * 8, c * LANES:(c + 1) * LANES] = (
                    hf_ref[c, pl.ds(t * TILE + i, 8, stride=SCAN_RUN), :])
    h_scan = hperm_ref[...].astype(BF16)

    def mm(lo, hi):
        return _dot(h, w_ref[:, lo:hi])

    c = 0
    mla_ref[...] = mm(c, c + MLA_U); c += MLA_U
    sbq_ref[...] = (mm(c, c + GROUP) * LOG2E).astype(BF16); c += GROUP
    sbk_ref[...] = _dot(h_scan, w_ref[:, c:c + GROUP]).astype(BF16); c += GROUP
    qf = mm(c, c + GROUP); c += GROUP
    mbqf_ref[...] = qf
    mbq_ref[...] = (qf * LOG2E).astype(BF16)
    kf = mm(c, c + GROUP); c += GROUP
    mbk_ref[...] = kf.astype(BF16)
    conv_ref[...] = mm(c, c + 2 * GROUP); c += 2 * GROUP
    gate_ref[...] = mm(c, c + D_MIX).astype(BF16)
    sb_vt = _dot_nt(wvt_ref[:GROUP, :], h_scan)
    mb_vt = _dot_nt(wvt_ref[GROUP:, :], h)
    for t in range(ROW_TILE // TILE):
        rows = slice(t * TILE, (t + 1) * TILE)
        kmean_ref[0, t] = jnp.mean(kf[rows], axis=0, keepdims=True)
        sbvt_ref[0, t] = sb_vt[:, rows].astype(BF16)
        mbvt_ref[0, t] = mb_vt[:, rows].astype(BF16)


def _in_proj(x2, g, w_main, w_vt, batch, seq):
    n = batch * seq
    nk = seq // TILE
    per_b = seq // ROW_TILE
    tpr = ROW_TILE // TILE
    row = lambda cols: pl.BlockSpec((ROW_TILE, cols), lambda i: (i, 0))
    full = lambda shp: pl.BlockSpec(shp, lambda i: (0,) * len(shp))
    vt_spec = pl.BlockSpec((1, tpr, GROUP, TILE), lambda i: (i // per_b, i % per_b, 0, 0))
    km_spec = pl.BlockSpec((1, tpr, 1, GROUP), lambda i: (i // per_b, i % per_b, 0, 0))
    sd = jax.ShapeDtypeStruct
    return pl.pallas_call(
        _in_proj_kernel,
        grid=(n // ROW_TILE,),
        in_specs=[row(D_MODEL), full((1, D_MODEL)), full((D_MODEL, MAIN_COLS)),
                  full((2 * GROUP, D_MODEL))],
        out_specs=[row(MLA_U), row(GROUP), row(GROUP), row(GROUP), row(GROUP), row(GROUP),
                   row(2 * GROUP), row(D_MIX), vt_spec, vt_spec, km_spec],
        out_shape=[sd((n, MLA_U), F32), sd((n, GROUP), BF16), sd((n, GROUP), BF16),
                   sd((n, GROUP), BF16), sd((n, GROUP), F32), sd((n, GROUP), BF16),
                   sd((n, 2 * GROUP), F32), sd((n, D_MIX), BF16),
                   sd((batch, nk, GROUP, TILE), BF16), sd((batch, nk, GROUP, TILE), BF16),
                   sd((batch, nk, 1, GROUP), F32)],
        scratch_shapes=[pltpu.VMEM((D_MODEL // LANES, ROW_TILE, LANES), F32),
                        pltpu.VMEM((ROW_TILE, D_MODEL), F32)],
        compiler_params=_params(1),
        name="in_proj",
    )(x2, g, w_main, w_vt)


def _mla_prep_kernel(u_ref, gq_ref, gkv_ref, wq_ref, wqs_ref, wkn_ref, wvt_ref,
                     cq_ref, sq_ref, ck_ref, sk_ref, q_ref, k_ref, vt_ref):
    u = u_ref[...]
    cq = u[:, 0:256]
    lane = lax.broadcasted_iota(jnp.int32, cq.shape, 1)
    msq = jnp.sum(jnp.where(lane < Q_RANK, cq * cq, 0.0), axis=-1, keepdims=True) * (1.0 / Q_RANK)
    cqn = (cq * lax.rsqrt(msq + EPS) * gq_ref[...]).astype(BF16)
    qa = _dot(cqn, wq_ref[...])
    qs = _dot(cqn, wqs_ref[...])
    ckv = u[:, 256:384]
    msk = jnp.mean(ckv * ckv, axis=-1, keepdims=True)
    ckvn = (ckv * lax.rsqrt(msk + EPS) * gkv_ref[...]).astype(BF16)
    kn = _dot(ckvn, wkn_ref[...])
    tail = u[:, LANES:2 * LANES]
    kr = tail * ck_ref[...] + pltpu.roll(tail, LANES - ROPE_DIM, 1) * sk_ref[...]
    cq_t = cq_ref[...]
    sq_t = sq_ref[...]
    for h in range(N_HEADS):
        sl = slice(h * MLA_HEAD_PAD, (h + 1) * MLA_HEAD_PAD)
        q_ref[0, h] = (qa[:, sl] * cq_t + qs[:, sl] * sq_t).astype(BF16)
        k_ref[0, h] = (kn[:, sl] + kr).astype(BF16)
    vt = _dot_nt(wvt_ref[...], ckvn)
    for t in range(ROW_TILE // TILE):
        vt_ref[0, t] = vt[:, t * TILE:(t + 1) * TILE].astype(BF16)


def _mla_prep(mla_u, gq, gkv, wq, wqs, wkn, wvt, cq, sq, ck, sk, batch, seq):
    nk = seq // TILE
    per_b = seq // ROW_TILE
    tpr = ROW_TILE // TILE
    full = lambda shp: pl.BlockSpec(shp, lambda b, i: (0,) * len(shp))
    tab = pl.BlockSpec((ROW_TILE, MLA_HEAD_PAD), lambda b, i: (i, 0))
    head_spec = pl.BlockSpec((1, N_HEADS, ROW_TILE, MLA_HEAD_PAD), lambda b, i: (b, 0, i, 0))
    sd = jax.ShapeDtypeStruct
    return pl.pallas_call(
        _mla_prep_kernel,
        grid=(batch, per_b),
        in_specs=[pl.BlockSpec((ROW_TILE, MLA_U), lambda b, i: (b * per_b + i, 0)),
                  full((1, 256)), full((1, KV_RANK)),
                  full((256, N_HEADS * MLA_HEAD_PAD)), full((256, N_HEADS * MLA_HEAD_PAD)),
                  full((KV_RANK, N_HEADS * MLA_HEAD_PAD)), full((GROUP, KV_RANK)),
                  tab, tab, tab, tab],
        out_specs=[head_spec, head_spec,
                   pl.BlockSpec((1, tpr, GROUP, TILE), lambda b, i: (b, i, 0, 0))],
        out_shape=[sd((batch, N_HEADS, seq, MLA_HEAD_PAD), BF16),
                   sd((batch, N_HEADS, seq, MLA_HEAD_PAD), BF16),
                   sd((batch, nk, GROUP, TILE), BF16)],
        compiler_params=_params(2),
        name="mla_prep",
    )(mla_u, gq, gkv, wq, wqs, wkn, wvt, cq, sq, ck, sk)


def _head_rows(h):
    return slice(h * HEAD_DIM, (h + 1) * HEAD_DIM)


def _head_cols(h):
    return slice(h * TILE, (h + 1) * TILE)


def _key_rows(j):
    return pl.ds(pl.multiple_of(j * TILE, TILE), TILE)


def _store_head_masked_q(q, qbd_ref):
    lane = lax.broadcasted_iota(jnp.int32, (TILE, GROUP), 1)
    for h in range(N_HEADS):
        in_head = (lane >= h * HEAD_DIM) & (lane < (h + 1) * HEAD_DIM)
        qbd_ref[h * TILE:(h + 1) * TILE, :] = jnp.where(in_head, q, jnp.zeros_like(q))


def _sweep_scratch():
    s_buf = pltpu.VMEM((TILE, N_HEADS * TILE), F32)
    p_buf = pltpu.VMEM((N_HEADS, TILE, TILE), BF16)
    stat = pltpu.VMEM((N_HEADS, 1, TILE), F32)
    return [s_buf, s_buf, p_buf, p_buf, stat, stat, stat, pltpu.VMEM((N_HEADS, ACC_ROWS, TILE), F32)]


def _softmax_sweep(qi, score_fn, adjust_fn, vt_ref, o_ref, scratch):
    s_a, s_b, p_a, p_b, al_a, al_b, m_ref, acc_ref = scratch
    row = lax.broadcasted_iota(jnp.int32, (TILE, TILE), 0)
    col = lax.broadcasted_iota(jnp.int32, (TILE, TILE), 1)
    ones_rows = jnp.ones((ONES_ROWS, TILE), BF16)

    def key_tile(t):
        if isinstance(t, int) and t == 0:
            return qi
        return jnp.minimum(t - 1, jnp.maximum(qi - 1, 0))

    def softmax(h, t, s_ref, p_ref, al_ref):
        first = isinstance(t, int) and t == 0
        s, pen = adjust_fn(h, key_tile(t), s_ref[:, _head_cols(h)], first)
        if first:
            s = jnp.where(row <= col, s, -jnp.inf)
            m_new = jnp.max(s, axis=0, keepdims=True)
            p = jnp.exp2(s - m_new)
        else:
            dead = jnp.where(t <= qi, 0.0, -jnp.inf)
            pen = dead if pen is None else pen + dead
            m_old = m_ref[h]
            m_new = jnp.maximum(m_old, jnp.max(s, axis=0, keepdims=True) + pen)
            alpha = jnp.exp2(m_old - m_new)
            p = jnp.exp2(s - (m_new - pen))
            al_ref[h] = alpha
        m_ref[h] = m_new
        p_ref[h] = p.astype(BF16)

    def weighted_values(h, t, p_ref, al_ref):
        first = isinstance(t, int) and t == 0
        vt = jnp.concatenate([vt_ref[0, key_tile(t), _head_rows(h), :], ones_rows], axis=0)
        pv = _dot(vt, p_ref[h])[:ACC_ROWS]
        acc_ref[h] = pv if first else acc_ref[h] * al_ref[h] + pv

    def half(t, cur, nxt):
        (s_cur, p_cur, al_cur), (s_nxt, p_nxt, al_nxt) = cur, nxt
        for h in range(N_HEADS):
            score_fn(h, key_tile(t + 1), s_nxt)
            if not (isinstance(t, int) and t == 0):
                weighted_values(h, t - 1, p_nxt, al_nxt)
            softmax(h, t, s_cur, p_cur, al_cur)

    buf_a, buf_b = (s_a, p_a, al_a), (s_b, p_b, al_b)
    for h in range(N_HEADS):
        score_fn(h, key_tile(0), s_a)
    half(0, buf_a, buf_b)
    half(1, buf_b, buf_a)

    def pair(k, carry):
        half(2 * k, buf_a, buf_b)
        half(2 * k + 1, buf_b, buf_a)
        return carry

    n_pairs = qi // 2 + 1
    lax.fori_loop(1, n_pairs, pair, 0)
    for h in range(N_HEADS):
        weighted_values(h, 2 * n_pairs - 1, p_b, al_b)
    out_t = [acc_ref[h, :HEAD_DIM, :] / acc_ref[h, HEAD_DIM:HEAD_DIM + 1, :] for h in range(N_HEADS)]
    o_ref[0] = jnp.concatenate(out_t, axis=0).T.astype(o_ref.dtype)


def _mla_attn_kernel(q_ref, k_ref, vt_ref, o_ref, *scratch):
    qi = pl.program_id(1)

    def score_fn(h, j, s_ref):
        s_ref[:, _head_cols(h)] = _dot_nt(k_ref[0, h, _key_rows(j), :], q_ref[0, h])

    _softmax_sweep(qi, score_fn, lambda h, j, s, first: (s, None), vt_ref, o_ref, scratch)


def _mla_attn(q, k, vt, batch, seq):
    nk = seq // TILE
    return pl.pallas_call(
        _mla_attn_kernel,
        grid=(batch, nk),
        in_specs=[pl.BlockSpec((1, N_HEADS, TILE, MLA_HEAD_PAD), lambda b, i: (b, 0, i, 0)),
                  pl.BlockSpec((1, N_HEADS, seq, MLA_HEAD_PAD), lambda b, i: (b, 0, 0, 0)),
                  pl.BlockSpec((1, nk, GROUP, TILE), lambda b, i: (b, 0, 0, 0))],
        out_specs=pl.BlockSpec((1, TILE, GROUP), lambda b, i: (b, i, 0)),
        out_shape=jax.ShapeDtypeStruct((batch, seq, GROUP), BF16),
        scratch_shapes=_sweep_scratch(),
        compiler_params=_params(2),
        name="mla_attn",
    )(q, k, vt)


def _split_bf16(x):
    hi = x.astype(BF16)
    lo = (x - hi.astype(F32)).astype(BF16)
    return hi, lo


def _suffix_scan(x, carry):
    slabs = [x[i * 8:(i + 1) * 8, :] for i in range(SCAN_RUN)]
    rest_of_run = [None] * SCAN_RUN
    rest_of_run[SCAN_RUN - 1] = slabs[SCAN_RUN - 1]
    for i in range(SCAN_RUN - 2, -1, -1):
        rest_of_run[i] = rest_of_run[i + 1] + slabs[i]
    run_total = rest_of_run[0]
    sub = lax.broadcasted_iota(jnp.int32, run_total.shape, 0)
    from_here = run_total
    for d in (1, 2, 4):
        from_here = from_here + jnp.where(sub + d < 8, pltpu.roll(from_here, 8 - d, 0), 0.0)
    later_runs = from_here - run_total + carry
    suffix = jnp.concatenate([rest_of_run[i] + later_runs for i in range(SCAN_RUN)], axis=0)
    return suffix, from_here[0:1, :]


def _sb_attn_kernel(q_ref, k_ref, vt_ref, o_ref, qbd_ref, carry_ref, acc_ref):
    qi = pl.program_id(1)
    _store_head_masked_q(q_ref[0], qbd_ref)
    row = lax.broadcasted_iota(jnp.int32, (TILE, TILE), 0)
    col = lax.broadcasted_iota(jnp.int32, (TILE, TILE), 1)
    token = (row & 7) * SCAN_RUN + (row >> 3)
    past = token < col

    def tile(j, diag):
        z_all = _dot_nt(k_ref[0, _key_rows(j), :], qbd_ref[...])
        for h in range(N_HEADS):
            hs = _head_rows(h)
            z = z_all[:, h * TILE:(h + 1) * TILE]
            neg_abs = lax.bitcast_convert_type(
                lax.bitcast_convert_type(z, jnp.uint32) | jnp.uint32(SIGN_BIT), F32)
            w = jnp.maximum(z, 0.0) + jnp.log2(1.0 + jnp.exp2(neg_abs))
            if diag:
                w = jnp.where(past, w, 0.0)
                w_from_here, tile_sum = _suffix_scan(w, jnp.zeros((1, TILE), F32))
                a = jnp.where(past, jnp.exp2(z - w_from_here), 0.0)
                carry_ref[h] = tile_sum
                acc_ref[hs, :] = _dot(vt_ref[0, j, hs, :], a.astype(BF16))
            else:
                carry = carry_ref[h]
                w_from_here, tile_sum = _suffix_scan(w, carry)
                a = jnp.exp2(z - w_from_here)
                carry_ref[h] = carry + tile_sum
                acc_ref[hs, :] = acc_ref[hs, :] + _dot(vt_ref[0, j, hs, :], a.astype(BF16))

    def live():
        return jnp.min(carry_ref[...]) < -SB_EXP2_UNDERFLOW

    tile(qi, True)

    def body(c):
        tile(qi - 1 - c[0], False)
        return c[0] + 1, live()

    lax.while_loop(lambda c: (c[0] < qi) & c[1], body, (jnp.int32(0), live()))
    o_ref[0] = acc_ref[...].T.astype(o_ref.dtype)


def _sb_attn(q, k, vt, batch, seq):
    nk = seq // TILE
    return pl.pallas_call(
        _sb_attn_kernel,
        grid=(batch, nk),
        in_specs=[pl.BlockSpec((1, TILE, GROUP), lambda b, i: (b, i, 0)),
                  pl.BlockSpec((1, seq, GROUP), lambda b, i: (b, 0, 0)),
                  pl.BlockSpec((1, nk, GROUP, TILE), lambda b, i: (b, 0, 0, 0))],
        out_specs=pl.BlockSpec((1, TILE, GROUP), lambda b, i: (b, i, 0)),
        out_shape=jax.ShapeDtypeStruct((batch, seq, GROUP), BF16),
        scratch_shapes=[pltpu.VMEM((N_HEADS * TILE, GROUP), BF16), pltpu.VMEM((N_HEADS, 1, TILE), F32),
                        pltpu.VMEM((GROUP, TILE), F32)],
        compiler_params=_params(2),
        name="sb_attn",
    )(q, k, vt)


def _moba_attn_kernel(nk, q_ref, qf_ref, k_ref, vt_ref, km_ref, bias_ref, o_ref,
                      qbd_ref, sel_ref, *scratch):
    qi = pl.program_id(1)
    _store_head_masked_q(q_ref[0], qbd_ref)
    qf = qf_ref[0]
    km_hi, km_lo = _split_bf16(km_ref[0, :, 0, :])
    blk = lax.broadcasted_iota(jnp.int32, (nk, TILE), 0)
    lane = lax.broadcasted_iota(jnp.int32, (TILE, GROUP), 1)
    for h in range(N_HEADS):
        in_head = (lane >= h * HEAD_DIM) & (lane < (h + 1) * HEAD_DIM)
        qf_hi, qf_lo = _split_bf16(jnp.where(in_head, qf, 0.0))
        gate = _dot_nt(km_hi, qf_hi) + (_dot_nt(km_hi, qf_lo) + _dot_nt(km_lo, qf_hi))
        beaten = jnp.zeros((nk, TILE), jnp.int32)
        for jp in range(nk):
            gj = gate[jp:jp + 1, :]
            beats = (gj > gate) | ((gj == gate) & (jp < blk))
            beaten = beaten + jnp.where(beats & (jp < qi), 1, 0)
        keep = (blk < qi) & (beaten < MOBA_TOPK)
        sel_ref[h] = jnp.where(keep, 0.0, -jnp.inf)

    def score_fn(h, j, s_ref):
        s_ref[:, _head_cols(h)] = _dot_nt(k_ref[0, _key_rows(j), :], qbd_ref[_head_cols(h), :])

    def adjust_fn(h, j, s, first):
        if first:
            return s + bias_ref[0, h], None
        d = jnp.minimum(qi - j, N_BIAS_TILES - 1)
        return s + bias_ref[d, h], sel_ref[h, pl.ds(j, 1), :]

    _softmax_sweep(qi, score_fn, adjust_fn, vt_ref, o_ref, scratch)


def _moba_attn(q, qf, k, vt, kmean, bias, batch, seq):
    nk = seq // TILE
    return pl.pallas_call(
        functools.partial(_moba_attn_kernel, nk),
        grid=(batch, nk),
        in_specs=[pl.BlockSpec((1, TILE, GROUP), lambda b, i: (b, i, 0)),
                  pl.BlockSpec((1, TILE, GROUP), lambda b, i: (b, i, 0)),
                  pl.BlockSpec((1, seq, GROUP), lambda b, i: (b, 0, 0)),
                  pl.BlockSpec((1, nk, GROUP, TILE), lambda b, i: (b, 0, 0, 0)),
                  pl.BlockSpec((1, nk, 1, GROUP), lambda b, i: (b, 0, 0, 0)),
                  pl.BlockSpec((N_BIAS_TILES, N_HEADS, TILE, TILE), lambda b, i: (0, 0, 0, 0))],
        out_specs=pl.BlockSpec((1, TILE, GROUP), lambda b, i: (b, i, 0)),
        out_shape=jax.ShapeDtypeStruct((batch, seq, GROUP), BF16),
        scratch_shapes=[pltpu.VMEM((N_HEADS * TILE, GROUP), BF16), pltpu.VMEM((N_HEADS, nk, TILE), F32)]
        + _sweep_scratch(),
        compiler_params=_params(2),
        name="moba_attn",
    )(q, qf, k, vt, kmean, bias)


def _conv_kernel(u_ref, halo_ref, dw_ref, dwb_ref, lng_ref, lnb_ref, pw_ref, pwb_ref, o_ref,
                 ext_ref, shift_ref):
    i = pl.program_id(1)
    um = u_ref[0]
    uh = halo_ref[0]
    xh = uh[:, :GROUP] * jax.nn.sigmoid(uh[:, GROUP:])
    ext_ref[0:CONV_HALO, :] = jnp.where(i == 0, 0.0, xh)
    ext_ref[CONV_HALO:, :] = um[:, :GROUP] * jax.nn.sigmoid(um[:, GROUP:])
    base = CONV_HALO - (CONV_WIDTH - 1)
    for b in range(1, 8):
        shift_ref[b - 1] = ext_ref[b:b + CONV_SHIFT_ROWS, :]
    y = jnp.zeros((ROW_TILE, GROUP), F32)
    for w in range(CONV_WIDTH):
        a, b = divmod(base + w, 8)
        rows = slice(8 * a, 8 * a + ROW_TILE)
        tap = ext_ref[rows, :] if b == 0 else shift_ref[b - 1, rows, :]
        y = y + tap * dw_ref[w:w + 1, :]
    y = y + dwb_ref[...]
    mu = jnp.mean(y, axis=-1, keepdims=True)
    yc = y - mu
    var = jnp.mean(yc * yc, axis=-1, keepdims=True)
    yn = yc * lax.rsqrt(var + EPS) * lng_ref[...] + lnb_ref[...]
    sw = yn * jax.nn.sigmoid(yn)
    o_ref[0] = (_dot(sw.astype(BF16), pw_ref[...]) + pwb_ref[...]).astype(o_ref.dtype)


def _conv_module(u_conv, dw, dwb, lng, lnb, pw, pwb, batch, seq):
    per_b = seq // ROW_TILE
    halo_per_tile = ROW_TILE // CONV_HALO
    full = lambda shp: pl.BlockSpec(shp, lambda b, i: (0,) * len(shp))
    return pl.pallas_call(
        _conv_kernel,
        grid=(batch, per_b),
        in_specs=[pl.BlockSpec((1, ROW_TILE, 2 * GROUP), lambda b, i: (b, i, 0)),
                  pl.BlockSpec((1, CONV_HALO, 2 * GROUP),
                               lambda b, i: (b, jnp.maximum(i * halo_per_tile - 1, 0), 0)),
                  full((CONV_HALO, GROUP)), full((1, GROUP)), full((1, GROUP)), full((1, GROUP)),
                  full((GROUP, GROUP)), full((1, GROUP))],
        out_specs=pl.BlockSpec((1, ROW_TILE, GROUP), lambda b, i: (b, i, 0)),
        out_shape=jax.ShapeDtypeStruct((batch, seq, GROUP), BF16),
        scratch_shapes=[pltpu.VMEM((CONV_HALO + ROW_TILE, GROUP), F32),
                        pltpu.VMEM((7, CONV_SHIFT_ROWS, GROUP), F32)],
        compiler_params=_params(2),
        name="conv_module",
    )(u_conv, u_conv, dw, dwb, lng, lnb, pw, pwb)


def _out_proj_kernel(oa_ref, ob_ref, oc_ref, od_ref, gate_ref, w_ref, g_ref, x_ref, o_ref):
    gt = gate_ref[...].astype(F32)
    sg = gt * jax.nn.sigmoid(gt)
    mix = jnp.concatenate([oa_ref[...], ob_ref[...], oc_ref[...], od_ref[...]], axis=-1).astype(F32)
    y = _dot((mix * sg).astype(BF16), w_ref[...])
    ms = jnp.mean(y * y, axis=-1, keepdims=True)
    o_ref[...] = x_ref[...] + y * lax.rsqrt(ms + EPS) * g_ref[...]


def _out_proj(oa, ob, oc, od, gate, w, g, x2):
    n = x2.shape[0]
    row = lambda cols: pl.BlockSpec((ROW_TILE, cols), lambda i: (i, 0))
    full = lambda shp: pl.BlockSpec(shp, lambda i: (0,) * len(shp))
    return pl.pallas_call(
        _out_proj_kernel,
        grid=(n // ROW_TILE,),
        in_specs=[row(GROUP), row(GROUP), row(GROUP), row(GROUP), row(D_MIX),
                  full((D_MIX, D_MODEL)), full((1, D_MODEL)), row(D_MODEL)],
        out_specs=row(D_MODEL),
        out_shape=jax.ShapeDtypeStruct((n, D_MODEL), F32),
        compiler_params=_params(1),
        name="out_proj",
    )(oa, ob, oc, od, gate, w, g, x2)


def _prep_in_proj_weights(w_in):
    c = 0
    cq = w_in[..., c:c + Q_RANK]; c += Q_RANK
    ckv = w_in[..., c:c + KV_RANK]; c += KV_RANK
    kr = w_in[..., c:c + ROPE_DIM]; c += ROPE_DIM
    sbq, sbk, sbv = (w_in[..., c + i * GROUP:c + (i + 1) * GROUP] for i in range(3)); c += 3 * GROUP
    mbq, mbk, mbv = (w_in[..., c + i * GROUP:c + (i + 1) * GROUP] for i in range(3)); c += 3 * GROUP
    conv = w_in[..., c:c + 2 * GROUP]; c += 2 * GROUP
    gate = w_in[..., c:c + D_MIX]
    kr_swapped = jnp.concatenate([kr[..., ROPE_HALF:], kr[..., :ROPE_HALF]], axis=-1)
    scale = HEAD_DIM ** -0.5
    main = jnp.concatenate([cq, kr, kr_swapped, ckv, sbq * scale, sbk, mbq * scale, mbk, conv, gate], axis=-1)
    vt = jnp.concatenate([jnp.swapaxes(sbv, 1, 2), jnp.swapaxes(mbv, 1, 2)], axis=1)
    return main.astype(BF16), vt.astype(BF16)


def _prep_mla_weights(w_uq, w_ukv):
    depth = w_uq.shape[0]
    qh = w_uq.reshape(depth, Q_RANK, N_HEADS, HEAD_DIM + ROPE_DIM)
    nope, r1, r2 = qh[..., :HEAD_DIM], qh[..., HEAD_DIM:HEAD_DIM + ROPE_HALF], qh[..., HEAD_DIM + ROPE_HALF:]
    zpad = jnp.zeros((depth, Q_RANK, N_HEADS, MLA_HEAD_PAD - HEAD_DIM - ROPE_DIM), w_uq.dtype)
    row_pad = ((0, 0), (0, 256 - Q_RANK), (0, 0))
    wq = jnp.pad(jnp.concatenate([nope, r1, r2, zpad], -1).reshape(depth, Q_RANK, -1), row_pad)
    wqs = jnp.pad(jnp.concatenate([jnp.zeros_like(nope), r2, r1, zpad], -1).reshape(depth, Q_RANK, -1), row_pad)
    kvh = w_ukv.reshape(depth, KV_RANK, N_HEADS, 2 * HEAD_DIM)
    k_nope, v = kvh[..., :HEAD_DIM], kvh[..., HEAD_DIM:]
    wkn = jnp.concatenate([k_nope, jnp.zeros_like(k_nope)], -1).reshape(depth, KV_RANK, -1)
    wvt = jnp.swapaxes(v.reshape(depth, KV_RANK, GROUP), 1, 2)
    return wq.astype(BF16), wqs.astype(BF16), wkn.astype(BF16), wvt.astype(BF16)


def _rope_tables(seq):
    freqs = ROPE_THETA ** (-jnp.arange(ROPE_HALF, dtype=F32) / ROPE_HALF)
    ang = jnp.arange(seq, dtype=jnp.int32).astype(F32)[:, None] * freqs[None, :]
    cos, sin = jnp.cos(ang), jnp.sin(ang)
    ones, zeros = jnp.ones((seq, HEAD_DIM), F32), jnp.zeros((seq, HEAD_DIM), F32)
    ztail = jnp.zeros((seq, MLA_HEAD_PAD - HEAD_DIM - ROPE_DIM), F32)
    c_tab = jnp.concatenate([ones, cos, cos, ztail], axis=-1)
    s_tab = jnp.concatenate([zeros, -sin, sin, ztail], axis=-1)
    ck_tab = jnp.concatenate([zeros, cos, cos, ztail], axis=-1)
    return c_tab, s_tab, ck_tab


def kernel(x, pre_norm_g, w_in, mla_q_norm_g, mla_w_uq, mla_kv_norm_g, mla_w_ukv, rel_bias, conv_dw_w, conv_dw_b, conv_ln_g, conv_ln_b, conv_pw_w, conv_pw_b, w_out, post_norm_g):
    batch, seq, d_model = x.shape
    depth = w_in.shape[0]
    assert d_model == D_MODEL and seq % ROW_TILE == 0 and ROW_TILE % TILE == 0
    n = batch * seq

    w_main, w_vt = _prep_in_proj_weights(w_in)
    wq, wqs, wkn, wvt_mla = _prep_mla_weights(mla_w_uq, mla_w_ukv)
    c_tab, s_tab, ck_tab = _rope_tables(seq)
    mla_scale = (HEAD_DIM + ROPE_DIM) ** -0.5 * LOG2E
    cq_tab, sq_tab = c_tab * mla_scale, s_tab * mla_scale
    gq = jnp.pad(mla_q_norm_g, ((0, 0), (0, 256 - Q_RANK)))
    dw = jnp.pad(conv_dw_w, ((0, 0), (0, CONV_HALO - CONV_WIDTH), (0, 0)))
    pw = conv_pw_w.astype(BF16)
    w_o = w_out.astype(BF16)
    bias = _bias_tiles(rel_bias)

    x2 = x.reshape(n, D_MODEL)
    for l in range(depth):
        (mla_u, sbq, sbk, mbq, mbqf, mbk, u_conv, u_gate, sbvt, mbvt, kmean) = _in_proj(
            x2, pre_norm_g[l][None], w_main[l], w_vt[l], batch, seq)
        qcat, kcat, mla_vt = _mla_prep(mla_u, gq[l][None], mla_kv_norm_g[l][None], wq[l], wqs[l], wkn[l],
                                       wvt_mla[l], cq_tab, sq_tab, ck_tab, s_tab, batch, seq)
        o_a = _mla_attn(qcat, kcat, mla_vt, batch, seq)
        o_b = _sb_attn(sbq.reshape(batch, seq, GROUP), sbk.reshape(batch, seq, GROUP), sbvt, batch, seq)
        o_c = _moba_attn(mbq.reshape(batch, seq, GROUP), mbqf.reshape(batch, seq, GROUP),
                         mbk.reshape(batch, seq, GROUP), mbvt, kmean, bias, batch, seq)
        o_d = _conv_module(u_conv.reshape(batch, seq, 2 * GROUP), dw[l], conv_dw_b[l][None],
                           conv_ln_g[l][None], conv_ln_b[l][None], pw[l], conv_pw_b[l][None], batch, seq)
        x2 = _out_proj(o_a.reshape(n, GROUP), o_b.reshape(n, GROUP), o_c.reshape(n, GROUP),
                       o_d.reshape(n, GROUP), u_gate, w_o[l], post_norm_g[l][None], x2)
    return x2.reshape(batch, seq, D_MODEL)
```

```python
import functools
import math

import jax
import jax.numpy as jnp
from jax import lax
from jax.experimental import pallas as pl
from jax.experimental.pallas import tpu as pltpu

F32 = jnp.float32
BF16 = jnp.bfloat16

D_MODEL = 1024
HEAD_DIM = 64
N_HEADS = 4
GROUP = N_HEADS * HEAD_DIM
D_MIX = 4 * GROUP
Q_RANK = 192
KV_RANK = 128
ROPE_DIM = 32
ROPE_HALF = ROPE_DIM // 2
ROPE_THETA = 10000.0
MLA_HEAD_PAD = 128
CONV_WIDTH = 31
NUM_BUCKETS = 32
MAX_DISTANCE = 1024
MOBA_TOPK = 3
EPS = 1e-6

LANES = 128
TILE = 256
ROW_TILE = 512
CONV_HALO = 32
CONV_SHIFT_ROWS = CONV_HALO + ROW_TILE - 8
N_BIAS_TILES = 6
MLA_U = 384
MAIN_COLS = MLA_U + 4 * GROUP + 2 * GROUP + D_MIX
VMEM_LIMIT = 56 * 1024 * 1024
SB_EXP2_UNDERFLOW = -150.0
SIGN_BIT = 0x80000000
SCAN_RUN = TILE // 8
ONES_ROWS = 16
ACC_ROWS = HEAD_DIM + 8
LOG2E = math.log2(math.e)

_NT = (((1,), (1,)), ((), ()))


def _params(n_axes):
    return pltpu.CompilerParams(dimension_semantics=("arbitrary",) * n_axes,
                                vmem_limit_bytes=VMEM_LIMIT)


def _dot(a, b):
    return jnp.dot(a, b, preferred_element_type=F32)


def _dot_nt(a, b):
    return lax.dot_general(a, b, _NT, preferred_element_type=F32)


def _bias_tiles_kernel(rb_ref, o_ref):
    d = pl.program_id(0)
    row = lax.broadcasted_iota(jnp.int32, (TILE, TILE), 0)
    col = lax.broadcasted_iota(jnp.int32, (TILE, TILE), 1)
    n = jnp.maximum(d * TILE + col - row, 0)
    max_exact = NUM_BUCKETS // 2
    n_large = jnp.maximum(n, max_exact).astype(F32)
    large = max_exact + (jnp.log(n_large / max_exact) / math.log(MAX_DISTANCE / max_exact)
                         * (NUM_BUCKETS - max_exact)).astype(jnp.int32)
    large = jnp.minimum(large, NUM_BUCKETS - 1)
    bucket = jnp.where(n < max_exact, n, large)
    for h in range(N_HEADS):
        acc = jnp.zeros((TILE, TILE), F32)
        for b in range(NUM_BUCKETS):
            acc = jnp.where(bucket == b, rb_ref[b, h], acc)
        o_ref[0, h] = acc * LOG2E


def _bias_tiles(rel_bias):
    return pl.pallas_call(
        _bias_tiles_kernel,
        grid=(N_BIAS_TILES,),
        in_specs=[pl.BlockSpec(memory_space=pltpu.SMEM)],
        out_specs=pl.BlockSpec((1, N_HEADS, TILE, TILE), lambda d: (d, 0, 0, 0)),
        out_shape=jax.ShapeDtypeStruct((N_BIAS_TILES, N_HEADS, TILE, TILE), F32),
        compiler_params=_params(1),
        name="t5_bias_tiles",
    )(rel_bias)


def _in_proj_kernel(x_ref, g_ref, w_ref, wvt_ref, *rest):
    mla_params, rest = rest[:10], rest[10:]
    (mlaq_ref, mlak_ref, mlavt_ref, sbq_ref, sbk_ref, mbq_ref, mbqf_ref, mbk_ref, conv_ref, gate_ref,
     sbvt_ref, mbvt_ref, kmean_ref, hf_ref, hperm_ref) = rest
    x = x_ref[...]
    ms = jnp.mean(x * x, axis=-1, keepdims=True)
    hf = x * lax.rsqrt(ms + EPS) * g_ref[...]
    h = hf.astype(BF16)
    for c in range(D_MODEL // LANES):
        hf_ref[c] = hf[:, c * LANES:(c + 1) * LANES]
    for c in range(D_MODEL // LANES):
        for t in range(ROW_TILE // TILE):
            for i in range(SCAN_RUN):
                hperm_ref[t * TILE + i * 8:t * TILE + (i + 1) * 8, c * LANES:(c + 1) * LANES] = (
                    hf_ref[c, pl.ds(t * TILE + i, 8, stride=SCAN_RUN), :])
    h_scan = hperm_ref[...].astype(BF16)

    def mm(lo, hi):
        return _dot(h, w_ref[:, lo:hi])

    c = 0
    _mla_prep(mm(c, c + MLA_U), *mla_params, mlaq_ref, mlak_ref, mlavt_ref); c += MLA_U
    sbq_ref[...] = (mm(c, c + GROUP) * LOG2E).astype(BF16); c += GROUP
    sbk_ref[...] = _dot(h_scan, w_ref[:, c:c + GROUP]).astype(BF16); c += GROUP
    qf = mm(c, c + GROUP); c += GROUP
    mbqf_ref[...] = qf
    mbq_ref[...] = (qf * LOG2E).astype(BF16)
    kf = mm(c, c + GROUP); c += GROUP
    mbk_ref[...] = kf.astype(BF16)
    conv_ref[...] = mm(c, c + 2 * GROUP); c += 2 * GROUP
    gate_ref[...] = mm(c, c + D_MIX).astype(BF16)
    sb_vt = _dot_nt(wvt_ref[:GROUP, :], h_scan)
    mb_vt = _dot_nt(wvt_ref[GROUP:, :], h)
    for t in range(ROW_TILE // TILE):
        rows = slice(t * TILE, (t + 1) * TILE)
        kmean_ref[0, t] = jnp.mean(kf[rows], axis=0, keepdims=True)
        sbvt_ref[0, t] = sb_vt[:, rows].astype(BF16)
        mbvt_ref[0, t] = mb_vt[:, rows].astype(BF16)


def _in_proj(x2, g, w_main, w_vt, mla_params, batch, seq):
    n = batch * seq
    nk = seq // TILE
    per_b = seq // ROW_TILE
    tpr = ROW_TILE // TILE
    row = lambda cols: pl.BlockSpec((ROW_TILE, cols), lambda i: (i, 0))
    full = lambda shp: pl.BlockSpec(shp, lambda i: (0,) * len(shp))
    tab = pl.BlockSpec((ROW_TILE, MLA_HEAD_PAD), lambda i: (i % per_b, 0))
    head_spec = pl.BlockSpec((1, N_HEADS, ROW_TILE, MLA_HEAD_PAD), lambda i: (i // per_b, 0, i % per_b, 0))
    vt_spec = pl.BlockSpec((1, tpr, GROUP, TILE), lambda i: (i // per_b, i % per_b, 0, 0))
    km_spec = pl.BlockSpec((1, tpr, 1, GROUP), lambda i: (i // per_b, i % per_b, 0, 0))
    sd = jax.ShapeDtypeStruct
    heads = sd((batch, N_HEADS, seq, MLA_HEAD_PAD), BF16)
    vt_tiles = sd((batch, nk, GROUP, TILE), BF16)
    return pl.pallas_call(
        _in_proj_kernel,
        grid=(n // ROW_TILE,),
        in_specs=[row(D_MODEL), full((1, D_MODEL)), full((D_MODEL, MAIN_COLS)),
                  full((2 * GROUP, D_MODEL)),
                  full((1, 256)), full((1, KV_RANK)),
                  full((256, N_HEADS * MLA_HEAD_PAD)), full((256, N_HEADS * MLA_HEAD_PAD)),
                  full((KV_RANK, N_HEADS * MLA_HEAD_PAD)), full((GROUP, KV_RANK)),
                  tab, tab, tab, tab],
        out_specs=[head_spec, head_spec, vt_spec,
                   row(GROUP), row(GROUP), row(GROUP), row(GROUP), row(GROUP),
                   row(2 * GROUP), row(D_MIX), vt_spec, vt_spec, km_spec],
        out_shape=[heads, heads, vt_tiles,
                   sd((n, GROUP), BF16), sd((n, GROUP), BF16),
                   sd((n, GROUP), BF16), sd((n, GROUP), F32), sd((n, GROUP), BF16),
                   sd((n, 2 * GROUP), F32), sd((n, D_MIX), BF16),
                   vt_tiles, vt_tiles, sd((batch, nk, 1, GROUP), F32)],
        scratch_shapes=[pltpu.VMEM((D_MODEL // LANES, ROW_TILE, LANES), F32),
                        pltpu.VMEM((ROW_TILE, D_MODEL), F32)],
        compiler_params=_params(1),
        name="in_proj",
    )(x2, g, w_main, w_vt, *mla_params)


def _mla_prep(u, gq_ref, gkv_ref, wq_ref, wqs_ref, wkn_ref, wvt_ref,
              cq_ref, sq_ref, ck_ref, sk_ref, q_ref, k_ref, vt_ref):
    cq = u[:, 0:256]
    lane = lax.broadcasted_iota(jnp.int32, cq.shape, 1)
    msq = jnp.sum(jnp.where(lane < Q_RANK, cq * cq, 0.0), axis=-1, keepdims=True) * (1.0 / Q_RANK)
    cqn = (cq * lax.rsqrt(msq + EPS) * gq_ref[...]).astype(BF16)
    qa = _dot(cqn, wq_ref[...])
    qs = _dot(cqn, wqs_ref[...])
    ckv = u[:, 256:384]
    msk = jnp.mean(ckv * ckv, axis=-1, keepdims=True)
    ckvn = (ckv * lax.rsqrt(msk + EPS) * gkv_ref[...]).astype(BF16)
    kn = _dot(ckvn, wkn_ref[...])
    tail = u[:, LANES:2 * LANES]
    kr = tail * ck_ref[...] + pltpu.roll(tail, LANES - ROPE_DIM, 1) * sk_ref[...]
    cq_t = cq_ref[...]
    sq_t = sq_ref[...]
    for h in range(N_HEADS):
        sl = slice(h * MLA_HEAD_PAD, (h + 1) * MLA_HEAD_PAD)
        q_ref[0, h] = (qa[:, sl] * cq_t + qs[:, sl] * sq_t).astype(BF16)
        k_ref[0, h] = (kn[:, sl] + kr).astype(BF16)
    vt = _dot_nt(wvt_ref[...], ckvn)
    for t in range(ROW_TILE // TILE):
        vt_ref[0, t] = vt[:, t * TILE:(t + 1) * TILE].astype(BF16)


def _head_rows(h):
    return slice(h * HEAD_DIM, (h + 1) * HEAD_DIM)


def _head_cols(h):
    return slice(h * TILE, (h + 1) * TILE)


def _key_rows(j):
    return pl.ds(pl.multiple_of(j * TILE, TILE), TILE)


def _store_head_masked_q(q, qbd_ref):
    lane = lax.broadcasted_iota(jnp.int32, (TILE, GROUP), 1)
    for h in range(N_HEADS):
        in_head = (lane >= h * HEAD_DIM) & (lane < (h + 1) * HEAD_DIM)
        qbd_ref[h * TILE:(h + 1) * TILE, :] = jnp.where(in_head, q, jnp.zeros_like(q))


def _sweep_scratch():
    s_buf = pltpu.VMEM((TILE, N_HEADS * TILE), F32)
    p_buf = pltpu.VMEM((N_HEADS, TILE, TILE), BF16)
    stat = pltpu.VMEM((N_HEADS, 1, TILE), F32)
    return [s_buf, s_buf, p_buf, p_buf, stat, stat, stat, pltpu.VMEM((N_HEADS, ACC_ROWS, TILE), F32)]


def _softmax_sweep(qi, score_fn, adjust_fn, vt_ref, o_ref, scratch):
    s_a, s_b, p_a, p_b, al_a, al_b, m_ref, acc_ref = scratch
    row = lax.broadcasted_iota(jnp.int32, (TILE, TILE), 0)
    col = lax.broadcasted_iota(jnp.int32, (TILE, TILE), 1)
    ones_rows = jnp.ones((ONES_ROWS, TILE), BF16)

    def key_tile(t):
        if isinstance(t, int) and t == 0:
            return qi
        return jnp.minimum(t - 1, jnp.maximum(qi - 1, 0))

    def softmax(h, t, s_ref, p_ref, al_ref):
        first = isinstance(t, int) and t == 0
        s, pen = adjust_fn(h, key_tile(t), s_ref[:, _head_cols(h)], first)
        if first:
            s = jnp.where(row <= col, s, -jnp.inf)
            m_new = jnp.max(s, axis=0, keepdims=True)
            p = jnp.exp2(s - m_new)
        else:
            dead = jnp.where(t <= qi, 0.0, -jnp.inf)
            pen = dead if pen is None else pen + dead
            m_old = m_ref[h]
            m_new = jnp.maximum(m_old, jnp.max(s, axis=0, keepdims=True) + pen)
            alpha = jnp.exp2(m_old - m_new)
            p = jnp.exp2(s - (m_new - pen))
            al_ref[h] = alpha
        m_ref[h] = m_new
        p_ref[h] = p.astype(BF16)

    def weighted_values(h, t, p_ref, al_ref):
        first = isinstance(t, int) and t == 0
        vt = jnp.concatenate([vt_ref[0, key_tile(t), _head_rows(h), :], ones_rows], axis=0)
        pv = _dot(vt, p_ref[h])[:ACC_ROWS]
        acc_ref[h] = pv if first else acc_ref[h] * al_ref[h] + pv

    def half(t, cur, nxt):
        (s_cur, p_cur, al_cur), (s_nxt, p_nxt, al_nxt) = cur, nxt
        for h in range(N_HEADS):
            score_fn(h, key_tile(t + 1), s_nxt)
            if not (isinstance(t, int) and t == 0):
                weighted_values(h, t - 1, p_nxt, al_nxt)
            softmax(h, t, s_cur, p_cur, al_cur)

    buf_a, buf_b = (s_a, p_a, al_a), (s_b, p_b, al_b)
    for h in range(N_HEADS):
        score_fn(h, key_tile(0), s_a)
    half(0, buf_a, buf_b)
    half(1, buf_b, buf_a)

    def pair(k, carry):
        half(2 * k, buf_a, buf_b)
        half(2 * k + 1, buf_b, buf_a)
        return carry

    n_pairs = qi // 2 + 1
    lax.fori_loop(1, n_pairs, pair, 0)
    for h in range(N_HEADS):
        weighted_values(h, 2 * n_pairs - 1, p_b, al_b)
    out_t = [acc_ref[h, :HEAD_DIM, :] / acc_ref[h, HEAD_DIM:HEAD_DIM + 1, :] for h in range(N_HEADS)]
    o_ref[0] = jnp.concatenate(out_t, axis=0).T.astype(o_ref.dtype)


def _mla_attn_kernel(q_ref, k_ref, vt_ref, o_ref, *scratch):
    qi = pl.program_id(1)

    def score_fn(h, j, s_ref):
        s_ref[:, _head_cols(h)] = _dot_nt(k_ref[0, h, _key_rows(j), :], q_ref[0, h])

    _softmax_sweep(qi, score_fn, lambda h, j, s, first: (s, None), vt_ref, o_ref, scratch)


def _mla_attn(q, k, vt, batch, seq):
    nk = seq // TILE
    return pl.pallas_call(
        _mla_attn_kernel,
        grid=(batch, nk),
        in_specs=[pl.BlockSpec((1, N_HEADS, TILE, MLA_HEAD_PAD), lambda b, i: (b, 0, i, 0)),
                  pl.BlockSpec((1, N_HEADS, seq, MLA_HEAD_PAD), lambda b, i: (b, 0, 0, 0)),
                  pl.BlockSpec((1, nk, GROUP, TILE), lambda b, i: (b, 0, 0, 0))],
        out_specs=pl.BlockSpec((1, TILE, GROUP), lambda b, i: (b, i, 0)),
        out_shape=jax.ShapeDtypeStruct((batch, seq, GROUP), BF16),
        scratch_shapes=_sweep_scratch(),
        compiler_params=_params(2),
        name="mla_attn",
    )(q, k, vt)


def _split_bf16(x):
    hi = x.astype(BF16)
    lo = (x - hi.astype(F32)).astype(BF16)
    return hi, lo


def _suffix_scan(x, carry):
    slabs = [x[i * 8:(i + 1) * 8, :] for i in range(SCAN_RUN)]
    rest_of_run = [None] * SCAN_RUN
    rest_of_run[SCAN_RUN - 1] = slabs[SCAN_RUN - 1]
    for i in range(SCAN_RUN - 2, -1, -1):
        rest_of_run[i] = rest_of_run[i + 1] + slabs[i]
    run_total = rest_of_run[0]
    sub = lax.broadcasted_iota(jnp.int32, run_total.shape, 0)
    from_here = run_total
    for d in (1, 2, 4):
        from_here = from_here + jnp.where(sub + d < 8, pltpu.roll(from_here, 8 - d, 0), 0.0)
    later_runs = from_here - run_total + carry
    suffix = jnp.concatenate([rest_of_run[i] + later_runs for i in range(SCAN_RUN)], axis=0)
    return suffix, from_here[0:1, :]


def _sb_attn_kernel(q_ref, k_ref, vt_ref, o_ref, qbd_ref, carry_ref, acc_ref):
    qi = pl.program_id(1)
    _store_head_masked_q(q_ref[0], qbd_ref)
    row = lax.broadcasted_iota(jnp.int32, (TILE, TILE), 0)
    col = lax.broadcasted_iota(jnp.int32, (TILE, TILE), 1)
    token = (row & 7) * SCAN_RUN + (row >> 3)
    past = token < col

    def tile(j, diag):
        z_all = _dot_nt(k_ref[0, _key_rows(j), :], qbd_ref[...])
        for h in range(N_HEADS):
            hs = _head_rows(h)
            z = z_all[:, h * TILE:(h + 1) * TILE]
            neg_abs = lax.bitcast_convert_type(
                lax.bitcast_convert_type(z, jnp.uint32) | jnp.uint32(SIGN_BIT), F32)
            w = jnp.maximum(z, 0.0) + jnp.log2(1.0 + jnp.exp2(neg_abs))
            if diag:
                w = jnp.where(past, w, 0.0)
                w_from_here, tile_sum = _suffix_scan(w, jnp.zeros((1, TILE), F32))
                a = jnp.where(past, jnp.exp2(z - w_from_here), 0.0)
                carry_ref[h] = tile_sum
                acc_ref[hs, :] = _dot(vt_ref[0, j, hs, :], a.astype(BF16))
            else:
                carry = carry_ref[h]
                w_from_here, tile_sum = _suffix_scan(w, carry)
                a = jnp.exp2(z - w_from_here)
                carry_ref[h] = carry + tile_sum
                acc_ref[hs, :] = acc_ref[hs, :] + _dot(vt_ref[0, j, hs, :], a.astype(BF16))

    def live():
        return jnp.min(carry_ref[...]) < -SB_EXP2_UNDERFLOW

    tile(qi, True)

    def body(c):
        tile(qi - 1 - c[0], False)
        return c[0] + 1, live()

    lax.while_loop(lambda c: (c[0] < qi) & c[1], body, (jnp.int32(0), live()))
    o_ref[0] = acc_ref[...].T.astype(o_ref.dtype)


def _sb_attn(q, k, vt, batch, seq):
    nk = seq // TILE
    return pl.pallas_call(
        _sb_attn_kernel,
        grid=(batch, nk),
        in_specs=[pl.BlockSpec((1, TILE, GROUP), lambda b, i: (b, i, 0)),
                  pl.BlockSpec((1, seq, GROUP), lambda b, i: (b, 0, 0)),
                  pl.BlockSpec((1, nk, GROUP, TILE), lambda b, i: (b, 0, 0, 0))],
        out_specs=pl.BlockSpec((1, TILE, GROUP), lambda b, i: (b, i, 0)),
        out_shape=jax.ShapeDtypeStruct((batch, seq, GROUP), BF16),
        scratch_shapes=[pltpu.VMEM((N_HEADS * TILE, GROUP), BF16), pltpu.VMEM((N_HEADS, 1, TILE), F32),
                        pltpu.VMEM((GROUP, TILE), F32)],
        compiler_params=_params(2),
        name="sb_attn",
    )(q, k, vt)


def _moba_attn_kernel(nk, q_ref, qf_ref, k_ref, vt_ref, km_ref, bias_ref, o_ref,
                      qbd_ref, sel_ref, *scratch):
    qi = pl.program_id(1)
    _store_head_masked_q(q_ref[0], qbd_ref)
    qf = qf_ref[0]
    km_hi, km_lo = _split_bf16(km_ref[0, :, 0, :])
    blk = lax.broadcasted_iota(jnp.int32, (nk, TILE), 0)
    lane = lax.broadcasted_iota(jnp.int32, (TILE, GROUP), 1)
    for h in range(N_HEADS):
        in_head = (lane >= h * HEAD_DIM) & (lane < (h + 1) * HEAD_DIM)
        qf_hi, qf_lo = _split_bf16(jnp.where(in_head, qf, 0.0))
        gate = _dot_nt(km_hi, qf_hi) + (_dot_nt(km_hi, qf_lo) + _dot_nt(km_lo, qf_hi))
        beaten = jnp.zeros((nk, TILE), jnp.int32)
        for jp in range(nk):
            gj = gate[jp:jp + 1, :]
            beats = (gj > gate) | ((gj == gate) & (jp < blk))
            beaten = beaten + jnp.where(beats & (jp < qi), 1, 0)
        keep = (blk < qi) & (beaten < MOBA_TOPK)
        sel_ref[h] = jnp.where(keep, 0.0, -jnp.inf)

    def score_fn(h, j, s_ref):
        s_ref[:, _head_cols(h)] = _dot_nt(k_ref[0, _key_rows(j), :], qbd_ref[_head_cols(h), :])

    def adjust_fn(h, j, s, first):
        if first:
            return s + bias_ref[0, h], None
        d = jnp.minimum(qi - j, N_BIAS_TILES - 1)
        return s + bias_ref[d, h], sel_ref[h, pl.ds(j, 1), :]

    _softmax_sweep(qi, score_fn, adjust_fn, vt_ref, o_ref, scratch)


def _moba_attn(q, qf, k, vt, kmean, bias, batch, seq):
    nk = seq // TILE
    return pl.pallas_call(
        functools.partial(_moba_attn_kernel, nk),
        grid=(batch, nk),
        in_specs=[pl.BlockSpec((1, TILE, GROUP), lambda b, i: (b, i, 0)),
                  pl.BlockSpec((1, TILE, GROUP), lambda b, i: (b, i, 0)),
                  pl.BlockSpec((1, seq, GROUP), lambda b, i: (b, 0, 0)),
                  pl.BlockSpec((1, nk, GROUP, TILE), lambda b, i: (b, 0, 0, 0)),
                  pl.BlockSpec((1, nk, 1, GROUP), lambda b, i: (b, 0, 0, 0)),
                  pl.BlockSpec((N_BIAS_TILES, N_HEADS, TILE, TILE), lambda b, i: (0, 0, 0, 0))],
        out_specs=pl.BlockSpec((1, TILE, GROUP), lambda b, i: (b, i, 0)),
        out_shape=jax.ShapeDtypeStruct((batch, seq, GROUP), BF16),
        scratch_shapes=[pltpu.VMEM((N_HEADS * TILE, GROUP), BF16), pltpu.VMEM((N_HEADS, nk, TILE), F32)]
        + _sweep_scratch(),
        compiler_params=_params(2),
        name="moba_attn",
    )(q, qf, k, vt, kmean, bias)


def _conv_module(i, u_ref, halo_ref, dw_ref, dwb_ref, lng_ref, lnb_ref, pw_ref, pwb_ref, ext_ref, shift_ref):
    um = u_ref[0]
    uh = halo_ref[0]
    xh = uh[:, :GROUP] * jax.nn.sigmoid(uh[:, GROUP:])
    ext_ref[0:CONV_HALO, :] = jnp.where(i == 0, 0.0, xh)
    ext_ref[CONV_HALO:, :] = um[:, :GROUP] * jax.nn.sigmoid(um[:, GROUP:])
    base = CONV_HALO - (CONV_WIDTH - 1)
    for b in range(1, 8):
        shift_ref[b - 1] = ext_ref[b:b + CONV_SHIFT_ROWS, :]
    y = jnp.zeros((ROW_TILE, GROUP), F32)
    for w in range(CONV_WIDTH):
        a, b = divmod(base + w, 8)
        rows = slice(8 * a, 8 * a + ROW_TILE)
        tap = ext_ref[rows, :] if b == 0 else shift_ref[b - 1, rows, :]
        y = y + tap * dw_ref[w:w + 1, :]
    y = y + dwb_ref[...]
    mu = jnp.mean(y, axis=-1, keepdims=True)
    yc = y - mu
    var = jnp.mean(yc * yc, axis=-1, keepdims=True)
    yn = yc * lax.rsqrt(var + EPS) * lng_ref[...] + lnb_ref[...]
    sw = yn * jax.nn.sigmoid(yn)
    return _dot(sw.astype(BF16), pw_ref[...]) + pwb_ref[...]


def _out_proj_kernel(u_ref, halo_ref, dw_ref, dwb_ref, lng_ref, lnb_ref, pw_ref, pwb_ref,
                     oa_ref, ob_ref, oc_ref, gate_ref, w_ref, g_ref, x_ref, o_ref, ext_ref, shift_ref):
    od = _conv_module(pl.program_id(1), u_ref, halo_ref, dw_ref, dwb_ref, lng_ref, lnb_ref, pw_ref, pwb_ref,
                      ext_ref, shift_ref)
    gt = gate_ref[0].astype(F32)
    sg = gt * jax.nn.sigmoid(gt)
    mix = jnp.concatenate([oa_ref[0].astype(F32), ob_ref[0].astype(F32), oc_ref[0].astype(F32), od], axis=-1)
    y = _dot((mix * sg).astype(BF16), w_ref[...])
    ms = jnp.mean(y * y, axis=-1, keepdims=True)
    o_ref[0] = x_ref[0] + y * lax.rsqrt(ms + EPS) * g_ref[...]


def _out_proj(u_conv, conv_params, oa, ob, oc, gate, w, g, x3):
    batch, seq, _ = x3.shape
    per_b = seq // ROW_TILE
    halo_per_tile = ROW_TILE // CONV_HALO
    row = lambda cols: pl.BlockSpec((1, ROW_TILE, cols), lambda b, i: (b, i, 0))
    full = lambda shp: pl.BlockSpec(shp, lambda b, i: (0,) * len(shp))
    return pl.pallas_call(
        _out_proj_kernel,
        grid=(batch, per_b),
        in_specs=[row(2 * GROUP),
                  pl.BlockSpec((1, CONV_HALO, 2 * GROUP),
                               lambda b, i: (b, jnp.maximum(i * halo_per_tile - 1, 0), 0)),
                  full((CONV_HALO, GROUP)), full((1, GROUP)), full((1, GROUP)), full((1, GROUP)),
                  full((GROUP, GROUP)), full((1, GROUP)),
                  row(GROUP), row(GROUP), row(GROUP), row(D_MIX),
                  full((D_MIX, D_MODEL)), full((1, D_MODEL)), row(D_MODEL)],
        out_specs=row(D_MODEL),
        out_shape=jax.ShapeDtypeStruct((batch, seq, D_MODEL), F32),
        scratch_shapes=[pltpu.VMEM((CONV_HALO + ROW_TILE, GROUP), F32),
                        pltpu.VMEM((7, CONV_SHIFT_ROWS, GROUP), F32)],
        compiler_params=_params(2),
        name="out_proj",
    )(u_conv, u_conv, *conv_params, oa, ob, oc, gate, w, g, x3)


def _prep_in_proj_weights(w_in):
    c = 0
    cq = w_in[..., c:c + Q_RANK]; c += Q_RANK
    ckv = w_in[..., c:c + KV_RANK]; c += KV_RANK
    kr = w_in[..., c:c + ROPE_DIM]; c += ROPE_DIM
    sbq, sbk, sbv = (w_in[..., c + i * GROUP:c + (i + 1) * GROUP] for i in range(3)); c += 3 * GROUP
    mbq, mbk, mbv = (w_in[..., c + i * GROUP:c + (i + 1) * GROUP] for i in range(3)); c += 3 * GROUP
    conv = w_in[..., c:c + 2 * GROUP]; c += 2 * GROUP
    gate = w_in[..., c:c + D_MIX]
    kr_swapped = jnp.concatenate([kr[..., ROPE_HALF:], kr[..., :ROPE_HALF]], axis=-1)
    scale = HEAD_DIM ** -0.5
    main = jnp.concatenate([cq, kr, kr_swapped, ckv, sbq * scale, sbk, mbq * scale, mbk, conv, gate], axis=-1)
    vt = jnp.concatenate([jnp.swapaxes(sbv, 1, 2), jnp.swapaxes(mbv, 1, 2)], axis=1)
    return main.astype(BF16), vt.astype(BF16)


def _prep_mla_weights(w_uq, w_ukv):
    depth = w_uq.shape[0]
    qh = w_uq.reshape(depth, Q_RANK, N_HEADS, HEAD_DIM + ROPE_DIM)
    nope, r1, r2 = qh[..., :HEAD_DIM], qh[..., HEAD_DIM:HEAD_DIM + ROPE_HALF], qh[..., HEAD_DIM + ROPE_HALF:]
    zpad = jnp.zeros((depth, Q_RANK, N_HEADS, MLA_HEAD_PAD - HEAD_DIM - ROPE_DIM), w_uq.dtype)
    row_pad = ((0, 0), (0, 256 - Q_RANK), (0, 0))
    wq = jnp.pad(jnp.concatenate([nope, r1, r2, zpad], -1).reshape(depth, Q_RANK, -1), row_pad)
    wqs = jnp.pad(jnp.concatenate([jnp.zeros_like(nope), r2, r1, zpad], -1).reshape(depth, Q_RANK, -1), row_pad)
    kvh = w_ukv.reshape(depth, KV_RANK, N_HEADS, 2 * HEAD_DIM)
    k_nope, v = kvh[..., :HEAD_DIM], kvh[..., HEAD_DIM:]
    wkn = jnp.concatenate([k_nope, jnp.zeros_like(k_nope)], -1).reshape(depth, KV_RANK, -1)
    wvt = jnp.swapaxes(v.reshape(depth, KV_RANK, GROUP), 1, 2)
    return wq.astype(BF16), wqs.astype(BF16), wkn.astype(BF16), wvt.astype(BF16)


def _rope_tables(seq):
    freqs = ROPE_THETA ** (-jnp.arange(ROPE_HALF, dtype=F32) / ROPE_HALF)
    ang = jnp.arange(seq, dtype=jnp.int32).astype(F32)[:, None] * freqs[None, :]
    cos, sin = jnp.cos(ang), jnp.sin(ang)
    ones, zeros = jnp.ones((seq, HEAD_DIM), F32), jnp.zeros((seq, HEAD_DIM), F32)
    ztail = jnp.zeros((seq, MLA_HEAD_PAD - HEAD_DIM - ROPE_DIM), F32)
    c_tab = jnp.concatenate([ones, cos, cos, ztail], axis=-1)
    s_tab = jnp.concatenate([zeros, -sin, sin, ztail], axis=-1)
    ck_tab = jnp.concatenate([zeros, cos, cos, ztail], axis=-1)
    return c_tab, s_tab, ck_tab


def kernel(x, pre_norm_g, w_in, mla_q_norm_g, mla_w_uq, mla_kv_norm_g, mla_w_ukv, rel_bias, conv_dw_w, conv_dw_b, conv_ln_g, conv_ln_b, conv_pw_w, conv_pw_b, w_out, post_norm_g):
    batch, seq, d_model = x.shape
    depth = w_in.shape[0]
    assert d_model == D_MODEL and seq % ROW_TILE == 0 and ROW_TILE % TILE == 0
    n = batch * seq

    w_main, w_vt = _prep_in_proj_weights(w_in)
    wq, wqs, wkn, wvt_mla = _prep_mla_weights(mla_w_uq, mla_w_ukv)
    c_tab, s_tab, ck_tab = _rope_tables(seq)
    mla_scale = (HEAD_DIM + ROPE_DIM) ** -0.5 * LOG2E
    cq_tab, sq_tab = c_tab * mla_scale, s_tab * mla_scale
    gq = jnp.pad(mla_q_norm_g, ((0, 0), (0, 256 - Q_RANK)))
    dw = jnp.pad(conv_dw_w, ((0, 0), (0, CONV_HALO - CONV_WIDTH), (0, 0)))
    pw = conv_pw_w.astype(BF16)
    w_o = w_out.astype(BF16)
    bias = _bias_tiles(rel_bias)

    seq_major = lambda a: a.reshape(batch, seq, a.shape[-1])
    for l in range(depth):
        mla_params = (gq[l][None], mla_kv_norm_g[l][None], wq[l], wqs[l], wkn[l], wvt_mla[l],
                      cq_tab, sq_tab, ck_tab, s_tab)
        (qcat, kcat, mla_vt, sbq, sbk, mbq, mbqf, mbk, u_conv, u_gate, sbvt, mbvt, kmean) = _in_proj(
            x.reshape(n, D_MODEL), pre_norm_g[l][None], w_main[l], w_vt[l], mla_params, batch, seq)
        o_a = _mla_attn(qcat, kcat, mla_vt, batch, seq)
        o_b = _sb_attn(seq_major(sbq), seq_major(sbk), sbvt, batch, seq)
        o_c = _moba_attn(seq_major(mbq), seq_major(mbqf), seq_major(mbk), mbvt, kmean, bias, batch, seq)
        conv_params = (dw[l], conv_dw_b[l][None], conv_ln_g[l][None], conv_ln_b[l][None], pw[l],
                       conv_pw_b[l][None])
        x = _out_proj(seq_major(u_conv), conv_params, o_a, o_b, o_c, seq_major(u_gate), w_o[l],
                      post_norm_g[l][None], x)
    return x
```

```python
import functools
import math

import jax
import jax.numpy as jnp
from jax import lax
from jax.experimental import pallas as pl
from jax.experimental.pallas import tpu as pltpu

F32 = jnp.float32
BF16 = jnp.bfloat16

D_MODEL = 1024
HEAD_DIM = 64
N_HEADS = 4
GROUP = N_HEADS * HEAD_DIM
D_MIX = 4 * GROUP
Q_RANK = 192
KV_RANK = 128
ROPE_DIM = 32
ROPE_HALF = ROPE_DIM // 2
ROPE_THETA = 10000.0
MLA_HEAD_PAD = 128
CONV_WIDTH = 31
NUM_BUCKETS = 32
MAX_DISTANCE = 1024
MOBA_TOPK = 3
EPS = 1e-6

LANES = 128
TILE = 256
ROW_TILE = 512
CONV_HALO = 32
CONV_SHIFT_ROWS = CONV_HALO + ROW_TILE - 8
N_BIAS_TILES = 6
MLA_U = 384
MAIN_COLS = MLA_U + 5 * GROUP + 2 * GROUP + D_MIX
VMEM_LIMIT = 56 * 1024 * 1024
SB_EXP2_UNDERFLOW = -150.0
SIGN_BIT = 0x80000000
SCAN_RUN = TILE // 8
ONES_ROWS = 16
ACC_ROWS = HEAD_DIM + 8
LOG2E = math.log2(math.e)

_NT = (((1,), (1,)), ((), ()))


def _params(n_axes):
    return pltpu.CompilerParams(dimension_semantics=("arbitrary",) * n_axes,
                                vmem_limit_bytes=VMEM_LIMIT)


def _dot(a, b):
    return jnp.dot(a, b, preferred_element_type=F32)


def _dot_nt(a, b):
    return lax.dot_general(a, b, _NT, preferred_element_type=F32)


def _bias_tiles_kernel(rb_ref, o_ref):
    d = pl.program_id(0)
    row = lax.broadcasted_iota(jnp.int32, (TILE, TILE), 0)
    col = lax.broadcasted_iota(jnp.int32, (TILE, TILE), 1)
    n = jnp.maximum(d * TILE + col - row, 0)
    max_exact = NUM_BUCKETS // 2
    n_large = jnp.maximum(n, max_exact).astype(F32)
    large = max_exact + (jnp.log(n_large / max_exact) / math.log(MAX_DISTANCE / max_exact)
                         * (NUM_BUCKETS - max_exact)).astype(jnp.int32)
    large = jnp.minimum(large, NUM_BUCKETS - 1)
    bucket = jnp.where(n < max_exact, n, large)
    for h in range(N_HEADS):
        acc = jnp.zeros((TILE, TILE), F32)
        for b in range(NUM_BUCKETS):
            acc = jnp.where(bucket == b, rb_ref[b, h], acc)
        o_ref[0, h] = acc * LOG2E


def _bias_tiles(rel_bias):
    return pl.pallas_call(
        _bias_tiles_kernel,
        grid=(N_BIAS_TILES,),
        in_specs=[pl.BlockSpec(memory_space=pltpu.SMEM)],
        out_specs=pl.BlockSpec((1, N_HEADS, TILE, TILE), lambda d: (d, 0, 0, 0)),
        out_shape=jax.ShapeDtypeStruct((N_BIAS_TILES, N_HEADS, TILE, TILE), F32),
        compiler_params=_params(1),
        name="t5_bias_tiles",
    )(rel_bias)


def _in_proj_kernel(x_ref, g_ref, w_ref, wvt_ref, *rest):
    mla_params, rest = rest[:10], rest[10:]
    (mlaq_ref, mlak_ref, mlavt_ref, sbq_ref, sbk_ref, mbq_ref, mbqf_ref, mbk_ref, conv_ref, gate_ref,
     sbvt_ref, mbvt_ref, kmean_ref, kv_ref, kvscan_ref) = rest
    x = x_ref[...]
    ms = jnp.mean(x * x, axis=-1, keepdims=True)
    h = (x * lax.rsqrt(ms + EPS) * g_ref[...]).astype(BF16)

    def mm(lo, hi):
        return _dot(h, w_ref[:, lo:hi])

    c = 0
    _mla_prep(mm(c, c + MLA_U), *mla_params, mlaq_ref, mlak_ref, mlavt_ref); c += MLA_U
    sbq_ref[...] = (mm(c, c + GROUP) * LOG2E).astype(BF16); c += GROUP
    kv = mm(c, c + 2 * GROUP); c += 2 * GROUP
    n_chunks = 2 * GROUP // LANES
    for cc in range(n_chunks):
        kv_ref[cc] = kv[:, cc * LANES:(cc + 1) * LANES]
    for cc in range(n_chunks):
        for t in range(ROW_TILE // TILE):
            for i in range(SCAN_RUN):
                kvscan_ref[t * TILE + i * 8:t * TILE + (i + 1) * 8, cc * LANES:(cc + 1) * LANES] = (
                    kv_ref[cc, pl.ds(t * TILE + i, 8, stride=SCAN_RUN), :])
    sbk_ref[...] = kvscan_ref[:, :GROUP].astype(BF16)
    for t in range(ROW_TILE // TILE):
        sbvt_ref[0, t] = kvscan_ref[t * TILE:(t + 1) * TILE, GROUP:].T.astype(BF16)
    qf = mm(c, c + GROUP); c += GROUP
    mbqf_ref[...] = qf
    mbq_ref[...] = (qf * LOG2E).astype(BF16)
    kf = mm(c, c + GROUP); c += GROUP
    mbk_ref[...] = kf.astype(BF16)
    conv_ref[...] = mm(c, c + 2 * GROUP); c += 2 * GROUP
    gate_ref[...] = mm(c, c + D_MIX).astype(BF16)
    mb_vt = _dot_nt(wvt_ref[...], h)
    for t in range(ROW_TILE // TILE):
        rows = slice(t * TILE, (t + 1) * TILE)
        kmean_ref[0, t] = jnp.mean(kf[rows], axis=0, keepdims=True)
        mbvt_ref[0, t] = mb_vt[:, rows].astype(BF16)


def _in_proj(x2, g, w_main, w_vt, mla_params, batch, seq):
    n = batch * seq
    nk = seq // TILE
    per_b = seq // ROW_TILE
    tpr = ROW_TILE // TILE
    row = lambda cols: pl.BlockSpec((ROW_TILE, cols), lambda i: (i, 0))
    full = lambda shp: pl.BlockSpec(shp, lambda i: (0,) * len(shp))
    tab = pl.BlockSpec((ROW_TILE, MLA_HEAD_PAD), lambda i: (i % per_b, 0))
    head_spec = pl.BlockSpec((1, N_HEADS, ROW_TILE, MLA_HEAD_PAD), lambda i: (i // per_b, 0, i % per_b, 0))
    vt_spec = pl.BlockSpec((1, tpr, GROUP, TILE), lambda i: (i // per_b, i % per_b, 0, 0))
    km_spec = pl.BlockSpec((1, tpr, 1, GROUP), lambda i: (i // per_b, i % per_b, 0, 0))
    sd = jax.ShapeDtypeStruct
    heads = sd((batch, N_HEADS, seq, MLA_HEAD_PAD), BF16)
    vt_tiles = sd((batch, nk, GROUP, TILE), BF16)
    return pl.pallas_call(
        _in_proj_kernel,
        grid=(n // ROW_TILE,),
        in_specs=[row(D_MODEL), full((1, D_MODEL)), full((D_MODEL, MAIN_COLS)),
                  full((GROUP, D_MODEL)),
                  full((1, 256)), full((1, KV_RANK)),
                  full((256, N_HEADS * MLA_HEAD_PAD)), full((256, N_HEADS * MLA_HEAD_PAD)),
                  full((KV_RANK, N_HEADS * MLA_HEAD_PAD)), full((GROUP, KV_RANK)),
                  tab, tab, tab, tab],
        out_specs=[head_spec, head_spec, vt_spec,
                   row(GROUP), row(GROUP), row(GROUP), row(GROUP), row(GROUP),
                   row(2 * GROUP), row(D_MIX), vt_spec, vt_spec, km_spec],
        out_shape=[heads, heads, vt_tiles,
                   sd((n, GROUP), BF16), sd((n, GROUP), BF16),
                   sd((n, GROUP), BF16), sd((n, GROUP), F32), sd((n, GROUP), BF16),
                   sd((n, 2 * GROUP), F32), sd((n, D_MIX), BF16),
                   vt_tiles, vt_tiles, sd((batch, nk, 1, GROUP), F32)],
        scratch_shapes=[pltpu.VMEM((2 * GROUP // LANES, ROW_TILE, LANES), F32),
                        pltpu.VMEM((ROW_TILE, 2 * GROUP), F32)],
        compiler_params=_params(1),
        name="in_proj",
    )(x2, g, w_main, w_vt, *mla_params)


def _mla_prep(u, gq_ref, gkv_ref, wq_ref, wqs_ref, wkn_ref, wvt_ref,
              cq_ref, sq_ref, ck_ref, sk_ref, q_ref, k_ref, vt_ref):
    cq = u[:, 0:256]
    lane = lax.broadcasted_iota(jnp.int32, cq.shape, 1)
    msq = jnp.sum(jnp.where(lane < Q_RANK, cq * cq, 0.0), axis=-1, keepdims=True) * (1.0 / Q_RANK)
    cqn = (cq * lax.rsqrt(msq + EPS) * gq_ref[...]).astype(BF16)
    qa = _dot(cqn, wq_ref[...])
    qs = _dot(cqn, wqs_ref[...])
    ckv = u[:, 256:384]
    msk = jnp.mean(ckv * ckv, axis=-1, keepdims=True)
    ckvn = (ckv * lax.rsqrt(msk + EPS) * gkv_ref[...]).astype(BF16)
    kn = _dot(ckvn, wkn_ref[...])
    tail = u[:, LANES:2 * LANES]
    kr = tail * ck_ref[...] + pltpu.roll(tail, LANES - ROPE_DIM, 1) * sk_ref[...]
    cq_t = cq_ref[...]
    sq_t = sq_ref[...]
    for h in range(N_HEADS):
        sl = slice(h * MLA_HEAD_PAD, (h + 1) * MLA_HEAD_PAD)
        q_ref[0, h] = (qa[:, sl] * cq_t + qs[:, sl] * sq_t).astype(BF16)
        k_ref[0, h] = (kn[:, sl] + kr).astype(BF16)
    vt = _dot_nt(wvt_ref[...], ckvn)
    for t in range(ROW_TILE // TILE):
        vt_ref[0, t] = vt[:, t * TILE:(t + 1) * TILE].astype(BF16)


def _head_rows(h):
    return slice(h * HEAD_DIM, (h + 1) * HEAD_DIM)


def _head_cols(h):
    return slice(h * TILE, (h + 1) * TILE)


def _key_rows(j):
    return pl.ds(pl.multiple_of(j * TILE, TILE), TILE)


def _store_head_masked_q(q, qbd_ref):
    lane = lax.broadcasted_iota(jnp.int32, (TILE, GROUP), 1)
    for h in range(N_HEADS):
        in_head = (lane >= h * HEAD_DIM) & (lane < (h + 1) * HEAD_DIM)
        qbd_ref[h * TILE:(h + 1) * TILE, :] = jnp.where(in_head, q, jnp.zeros_like(q))


def _sweep_scratch():
    s_buf = pltpu.VMEM((TILE, N_HEADS * TILE), F32)
    p_buf = pltpu.VMEM((N_HEADS, TILE, TILE), BF16)
    stat = pltpu.VMEM((N_HEADS, 1, TILE), F32)
    return [s_buf, s_buf, p_buf, p_buf, stat, stat, stat, pltpu.VMEM((N_HEADS, ACC_ROWS, TILE), F32)]


def _softmax_sweep(qi, score_fn, adjust_fn, vt_ref, o_ref, scratch):
    s_a, s_b, p_a, p_b, al_a, al_b, m_ref, acc_ref = scratch
    row = lax.broadcasted_iota(jnp.int32, (TILE, TILE), 0)
    col = lax.broadcasted_iota(jnp.int32, (TILE, TILE), 1)
    ones_rows = jnp.ones((ONES_ROWS, TILE), BF16)

    def key_tile(t):
        if isinstance(t, int) and t == 0:
            return qi
        return jnp.minimum(t - 1, jnp.maximum(qi - 1, 0))

    def softmax(h, t, s_ref, p_ref, al_ref):
        first = isinstance(t, int) and t == 0
        s, pen = adjust_fn(h, key_tile(t), s_ref[:, _head_cols(h)], first)
        if first:
            s = jnp.where(row <= col, s, -jnp.inf)
            m_new = jnp.max(s, axis=0, keepdims=True)
            p = jnp.exp2(s - m_new)
        else:
            dead = jnp.where(t <= qi, 0.0, -jnp.inf)
            pen = dead if pen is None else pen + dead
            m_old = m_ref[h]
            m_new = jnp.maximum(m_old, jnp.max(s, axis=0, keepdims=True) + pen)
            alpha = jnp.exp2(m_old - m_new)
            p = jnp.exp2(s - (m_new - pen))
            al_ref[h] = alpha
        m_ref[h] = m_new
        p_ref[h] = p.astype(BF16)

    def weighted_values(h, t, p_ref, al_ref):
        first = isinstance(t, int) and t == 0
        vt = jnp.concatenate([vt_ref[0, key_tile(t), _head_rows(h), :], ones_rows], axis=0)
        pv = _dot(vt, p_ref[h])[:ACC_ROWS]
        acc_ref[h] = pv if first else acc_ref[h] * al_ref[h] + pv

    def half(t, cur, nxt):
        (s_cur, p_cur, al_cur), (s_nxt, p_nxt, al_nxt) = cur, nxt
        for h in range(N_HEADS):
            score_fn(h, key_tile(t + 1), s_nxt)
            if not (isinstance(t, int) and t == 0):
                weighted_values(h, t - 1, p_nxt, al_nxt)
            softmax(h, t, s_cur, p_cur, al_cur)

    buf_a, buf_b = (s_a, p_a, al_a), (s_b, p_b, al_b)
    for h in range(N_HEADS):
        score_fn(h, key_tile(0), s_a)
    half(0, buf_a, buf_b)
    half(1, buf_b, buf_a)

    def pair(k, carry):
        half(2 * k, buf_a, buf_b)
        half(2 * k + 1, buf_b, buf_a)
        return carry

    def pairs_from(first, count):
        def body(k, carry):
            for i in range(count):
                pair(first + count * k + i, carry)
            return carry
        return body

    n_pairs = qi // 2 + 1
    n_quad = (n_pairs - 1) // 4
    n_double = ((n_pairs - 1) % 4) // 2
    lax.fori_loop(0, n_quad, pairs_from(1, 4), 0)
    lax.fori_loop(0, n_double, pairs_from(4 * n_quad + 1, 2), 0)
    lax.fori_loop(4 * n_quad + 2 * n_double + 1, n_pairs, pair, 0)
    for h in range(N_HEADS):
        weighted_values(h, 2 * n_pairs - 1, p_b, al_b)
    out_t = [acc_ref[h, :HEAD_DIM, :] / acc_ref[h, HEAD_DIM:HEAD_DIM + 1, :] for h in range(N_HEADS)]
    o_ref[0] = jnp.concatenate(out_t, axis=0).T.astype(o_ref.dtype)


def _mla_attn_kernel(q_ref, k_ref, vt_ref, o_ref, *scratch):
    qi = pl.program_id(1)

    def score_fn(h, j, s_ref):
        s_ref[:, _head_cols(h)] = _dot_nt(k_ref[0, h, _key_rows(j), :], q_ref[0, h])

    _softmax_sweep(qi, score_fn, lambda h, j, s, first: (s, None), vt_ref, o_ref, scratch)


def _mla_attn(q, k, vt, batch, seq):
    nk = seq // TILE
    return pl.pallas_call(
        _mla_attn_kernel,
        grid=(batch, nk),
        in_specs=[pl.BlockSpec((1, N_HEADS, TILE, MLA_HEAD_PAD), lambda b, i: (b, 0, i, 0)),
                  pl.BlockSpec((1, N_HEADS, seq, MLA_HEAD_PAD), lambda b, i: (b, 0, 0, 0)),
                  pl.BlockSpec((1, nk, GROUP, TILE), lambda b, i: (b, 0, 0, 0))],
        out_specs=pl.BlockSpec((1, TILE, GROUP), lambda b, i: (b, i, 0)),
        out_shape=jax.ShapeDtypeStruct((batch, seq, GROUP), BF16),
        scratch_shapes=_sweep_scratch(),
        compiler_params=_params(2),
        name="mla_attn",
    )(q, k, vt)


def _split_bf16(x):
    hi = x.astype(BF16)
    lo = (x - hi.astype(F32)).astype(BF16)
    return hi, lo


def _suffix_scan(x, carry):
    slabs = [x[i * 8:(i + 1) * 8, :] for i in range(SCAN_RUN)]
    rest_of_run = [None] * SCAN_RUN
    rest_of_run[SCAN_RUN - 1] = slabs[SCAN_RUN - 1]
    for i in range(SCAN_RUN - 2, -1, -1):
        rest_of_run[i] = rest_of_run[i + 1] + slabs[i]
    run_total = rest_of_run[0]
    sub = lax.broadcasted_iota(jnp.int32, run_total.shape, 0)
    from_here = run_total
    for d in (1, 2, 4):
        from_here = from_here + jnp.where(sub + d < 8, pltpu.roll(from_here, 8 - d, 0), 0.0)
    later_runs = from_here - run_total + carry
    suffix = jnp.concatenate([rest_of_run[i] + later_runs for i in range(SCAN_RUN)], axis=0)
    return suffix, from_here[0:1, :]


def _sb_attn_kernel(q_ref, k_ref, vt_ref, o_ref, qbd_ref, carry_ref, acc_ref):
    qi = pl.program_id(1)
    _store_head_masked_q(q_ref[0], qbd_ref)
    row = lax.broadcasted_iota(jnp.int32, (TILE, TILE), 0)
    col = lax.broadcasted_iota(jnp.int32, (TILE, TILE), 1)
    token = (row & 7) * SCAN_RUN + (row >> 3)
    past = token < col

    def tile(j, diag):
        z_all = _dot_nt(k_ref[0, _key_rows(j), :], qbd_ref[...])
        for h in range(N_HEADS):
            hs = _head_rows(h)
            z = z_all[:, h * TILE:(h + 1) * TILE]
            neg_abs = lax.bitcast_convert_type(
                lax.bitcast_convert_type(z, jnp.uint32) | jnp.uint32(SIGN_BIT), F32)
            w = jnp.maximum(z, 0.0) + jnp.log2(1.0 + jnp.exp2(neg_abs))
            if diag:
                w = jnp.where(past, w, 0.0)
                w_from_here, tile_sum = _suffix_scan(w, jnp.zeros((1, TILE), F32))
                a = jnp.where(past, jnp.exp2(z - w_from_here), 0.0)
                carry_ref[h] = tile_sum
                acc_ref[hs, :] = _dot(vt_ref[0, j, hs, :], a.astype(BF16))
            else:
                carry = carry_ref[h]
                w_from_here, tile_sum = _suffix_scan(w, carry)
                a = jnp.exp2(z - w_from_here)
                carry_ref[h] = carry + tile_sum
                acc_ref[hs, :] = acc_ref[hs, :] + _dot(vt_ref[0, j, hs, :], a.astype(BF16))

    def live():
        return jnp.min(carry_ref[...]) < -SB_EXP2_UNDERFLOW

    tile(qi, True)

    def body(c):
        tile(qi - 1 - c[0], False)
        return c[0] + 1, live()

    lax.while_loop(lambda c: (c[0] < qi) & c[1], body, (jnp.int32(0), live()))
    o_ref[0] = acc_ref[...].T.astype(o_ref.dtype)


def _sb_attn(q, k, vt, batch, seq):
    nk = seq // TILE
    return pl.pallas_call(
        _sb_attn_kernel,
        grid=(batch, nk),
        in_specs=[pl.BlockSpec((1, TILE, GROUP), lambda b, i: (b, i, 0)),
                  pl.BlockSpec((1, seq, GROUP), lambda b, i: (b, 0, 0)),
                  pl.BlockSpec((1, nk, GROUP, TILE), lambda b, i: (b, 0, 0, 0))],
        out_specs=pl.BlockSpec((1, TILE, GROUP), lambda b, i: (b, i, 0)),
        out_shape=jax.ShapeDtypeStruct((batch, seq, GROUP), BF16),
        scratch_shapes=[pltpu.VMEM((N_HEADS * TILE, GROUP), BF16), pltpu.VMEM((N_HEADS, 1, TILE), F32),
                        pltpu.VMEM((GROUP, TILE), F32)],
        compiler_params=_params(2),
        name="sb_attn",
    )(q, k, vt)


def _moba_attn_kernel(nk, q_ref, qf_ref, k_ref, vt_ref, km_ref, bias_ref, o_ref,
                      qbd_ref, sel_ref, *scratch):
    qi = pl.program_id(1)
    _store_head_masked_q(q_ref[0], qbd_ref)
    km = km_ref[0, :, 0, :]
    lane = lax.broadcasted_iota(jnp.int32, (nk, GROUP), 1)
    km_hi, km_lo = _split_bf16(jnp.concatenate(
        [jnp.where((lane >= h * HEAD_DIM) & (lane < (h + 1) * HEAD_DIM), km, 0.0) for h in range(N_HEADS)], axis=0))
    qf_hi, qf_lo = _split_bf16(qf_ref[0])
    gate_all = _dot_nt(km_hi, qf_hi) + (_dot_nt(km_hi, qf_lo) + _dot_nt(km_lo, qf_hi))
    blk = lax.broadcasted_iota(jnp.int32, (nk, TILE), 0)
    for h in range(N_HEADS):
        gate = gate_all[h * nk:(h + 1) * nk, :]
        rival = jnp.where(blk < qi, gate, -jnp.inf)
        beaten = jnp.zeros((nk, TILE), jnp.int32)
        for jp in range(nk):
            gj = rival[jp:jp + 1, :]
            beaten = beaten + jnp.where((gj > gate) | ((gj == gate) & (jp < blk)), 1, 0)
        keep = (blk < qi) & (beaten < MOBA_TOPK)
        sel_ref[h] = jnp.where(keep, 0.0, -jnp.inf)

    def score_fn(h, j, s_ref):
        s_ref[:, _head_cols(h)] = _dot_nt(k_ref[0, _key_rows(j), :], qbd_ref[_head_cols(h), :])

    def adjust_fn(h, j, s, first):
        if first:
            return s + bias_ref[0, h], None
        d = jnp.minimum(qi - j, N_BIAS_TILES - 1)
        return s + bias_ref[d, h], sel_ref[h, pl.ds(j, 1), :]

    _softmax_sweep(qi, score_fn, adjust_fn, vt_ref, o_ref, scratch)


def _moba_attn(q, qf, k, vt, kmean, bias, batch, seq):
    nk = seq // TILE
    return pl.pallas_call(
        functools.partial(_moba_attn_kernel, nk),
        grid=(batch, nk),
        in_specs=[pl.BlockSpec((1, TILE, GROUP), lambda b, i: (b, i, 0)),
                  pl.BlockSpec((1, TILE, GROUP), lambda b, i: (b, i, 0)),
                  pl.BlockSpec((1, seq, GROUP), lambda b, i: (b, 0, 0)),
                  pl.BlockSpec((1, nk, GROUP, TILE), lambda b, i: (b, 0, 0, 0)),
                  pl.BlockSpec((1, nk, 1, GROUP), lambda b, i: (b, 0, 0, 0)),
                  pl.BlockSpec((N_BIAS_TILES, N_HEADS, TILE, TILE), lambda b, i: (0, 0, 0, 0))],
        out_specs=pl.BlockSpec((1, TILE, GROUP), lambda b, i: (b, i, 0)),
        out_shape=jax.ShapeDtypeStruct((batch, seq, GROUP), BF16),
        scratch_shapes=[pltpu.VMEM((N_HEADS * TILE, GROUP), BF16), pltpu.VMEM((N_HEADS, nk, TILE), F32)]
        + _sweep_scratch(),
        compiler_params=_params(2),
        name="moba_attn",
    )(q, qf, k, vt, kmean, bias)


def _conv_module(i, u_ref, halo_ref, dw_ref, dwb_ref, lng_ref, lnb_ref, pw_ref, pwb_ref, ext_ref, shift_ref):
    um = u_ref[0]
    uh = halo_ref[0]
    xh = uh[:, :GROUP] * jax.nn.sigmoid(uh[:, GROUP:])
    ext_ref[0:CONV_HALO, :] = jnp.where(i == 0, 0.0, xh)
    ext_ref[CONV_HALO:, :] = um[:, :GROUP] * jax.nn.sigmoid(um[:, GROUP:])
    base = CONV_HALO - (CONV_WIDTH - 1)
    for b in range(1, 8):
        shift_ref[b - 1] = ext_ref[b:b + CONV_SHIFT_ROWS, :]
    y = jnp.zeros((ROW_TILE, GROUP), F32)
    for w in range(CONV_WIDTH):
        a, b = divmod(base + w, 8)
        rows = slice(8 * a, 8 * a + ROW_TILE)
        tap = ext_ref[rows, :] if b == 0 else shift_ref[b - 1, rows, :]
        y = y + tap * dw_ref[w:w + 1, :]
    y = y + dwb_ref[...]
    mu = jnp.mean(y, axis=-1, keepdims=True)
    yc = y - mu
    var = jnp.mean(yc * yc, axis=-1, keepdims=True)
    yn = yc * lax.rsqrt(var + EPS) * lng_ref[...] + lnb_ref[...]
    sw = yn * jax.nn.sigmoid(yn)
    return _dot(sw.astype(BF16), pw_ref[...]) + pwb_ref[...]


def _out_proj_kernel(u_ref, halo_ref, dw_ref, dwb_ref, lng_ref, lnb_ref, pw_ref, pwb_ref,
                     oa_ref, ob_ref, oc_ref, gate_ref, w_ref, g_ref, x_ref, o_ref, ext_ref, shift_ref):
    od = _conv_module(pl.program_id(1), u_ref, halo_ref, dw_ref, dwb_ref, lng_ref, lnb_ref, pw_ref, pwb_ref,
                      ext_ref, shift_ref)
    gt = gate_ref[0].astype(F32)
    sg = gt * jax.nn.sigmoid(gt)
    mix = jnp.concatenate([oa_ref[0].astype(F32), ob_ref[0].astype(F32), oc_ref[0].astype(F32), od], axis=-1)
    y = _dot((mix * sg).astype(BF16), w_ref[...])
    ms = jnp.mean(y * y, axis=-1, keepdims=True)
    o_ref[0] = x_ref[0] + y * lax.rsqrt(ms + EPS) * g_ref[...]


def _out_proj(u_conv, conv_params, oa, ob, oc, gate, w, g, x3):
    batch, seq, _ = x3.shape
    per_b = seq // ROW_TILE
    halo_per_tile = ROW_TILE // CONV_HALO
    row = lambda cols: pl.BlockSpec((1, ROW_TILE, cols), lambda b, i: (b, i, 0))
    full = lambda shp: pl.BlockSpec(shp, lambda b, i: (0,) * len(shp))
    return pl.pallas_call(
        _out_proj_kernel,
        grid=(batch, per_b),
        in_specs=[row(2 * GROUP),
                  pl.BlockSpec((1, CONV_HALO, 2 * GROUP),
                               lambda b, i: (b, jnp.maximum(i * halo_per_tile - 1, 0), 0)),
                  full((CONV_HALO, GROUP)), full((1, GROUP)), full((1, GROUP)), full((1, GROUP)),
                  full((GROUP, GROUP)), full((1, GROUP)),
                  row(GROUP), row(GROUP), row(GROUP), row(D_MIX),
                  full((D_MIX, D_MODEL)), full((1, D_MODEL)), row(D_MODEL)],
        out_specs=row(D_MODEL),
        out_shape=jax.ShapeDtypeStruct((batch, seq, D_MODEL), F32),
        scratch_shapes=[pltpu.VMEM((CONV_HALO + ROW_TILE, GROUP), F32),
                        pltpu.VMEM((7, CONV_SHIFT_ROWS, GROUP), F32)],
        compiler_params=_params(2),
        name="out_proj",
    )(u_conv, u_conv, *conv_params, oa, ob, oc, gate, w, g, x3)


def _prep_in_proj_weights(w_in):
    c = 0
    cq = w_in[..., c:c + Q_RANK]; c += Q_RANK
    ckv = w_in[..., c:c + KV_RANK]; c += KV_RANK
    kr = w_in[..., c:c + ROPE_DIM]; c += ROPE_DIM
    sbq, sbk, sbv = (w_in[..., c + i * GROUP:c + (i + 1) * GROUP] for i in range(3)); c += 3 * GROUP
    mbq, mbk, mbv = (w_in[..., c + i * GROUP:c + (i + 1) * GROUP] for i in range(3)); c += 3 * GROUP
    conv = w_in[..., c:c + 2 * GROUP]; c += 2 * GROUP
    gate = w_in[..., c:c + D_MIX]
    kr_swapped = jnp.concatenate([kr[..., ROPE_HALF:], kr[..., :ROPE_HALF]], axis=-1)
    scale = HEAD_DIM ** -0.5
    main = jnp.concatenate([cq, kr, kr_swapped, ckv, sbq * scale, sbk, sbv, mbq * scale, mbk, conv, gate], axis=-1)
    return main.astype(BF16), jnp.swapaxes(mbv, 1, 2).astype(BF16)


def _prep_mla_weights(w_uq, w_ukv):
    depth = w_uq.shape[0]
    qh = w_uq.reshape(depth, Q_RANK, N_HEADS, HEAD_DIM + ROPE_DIM)
    nope, r1, r2 = qh[..., :HEAD_DIM], qh[..., HEAD_DIM:HEAD_DIM + ROPE_HALF], qh[..., HEAD_DIM + ROPE_HALF:]
    zpad = jnp.zeros((depth, Q_RANK, N_HEADS, MLA_HEAD_PAD - HEAD_DIM - ROPE_DIM), w_uq.dtype)
    row_pad = ((0, 0), (0, 256 - Q_RANK), (0, 0))
    wq = jnp.pad(jnp.concatenate([nope, r1, r2, zpad], -1).reshape(depth, Q_RANK, -1), row_pad)
    wqs = jnp.pad(jnp.concatenate([jnp.zeros_like(nope), r2, r1, zpad], -1).reshape(depth, Q_RANK, -1), row_pad)
    kvh = w_ukv.reshape(depth, KV_RANK, N_HEADS, 2 * HEAD_DIM)
    k_nope, v = kvh[..., :HEAD_DIM], kvh[..., HEAD_DIM:]
    wkn = jnp.concatenate([k_nope, jnp.zeros_like(k_nope)], -1).reshape(depth, KV_RANK, -1)
    wvt = jnp.swapaxes(v.reshape(depth, KV_RANK, GROUP), 1, 2)
    return wq.astype(BF16), wqs.astype(BF16), wkn.astype(BF16), wvt.astype(BF16)


def _rope_tables(seq):
    freqs = ROPE_THETA ** (-jnp.arange(ROPE_HALF, dtype=F32) / ROPE_HALF)
    ang = jnp.arange(seq, dtype=jnp.int32).astype(F32)[:, None] * freqs[None, :]
    cos, sin = jnp.cos(ang), jnp.sin(ang)
    ones, zeros = jnp.ones((seq, HEAD_DIM), F32), jnp.zeros((seq, HEAD_DIM), F32)
    ztail = jnp.zeros((seq, MLA_HEAD_PAD - HEAD_DIM - ROPE_DIM), F32)
    c_tab = jnp.concatenate([ones, cos, cos, ztail], axis=-1)
    s_tab = jnp.concatenate([zeros, -sin, sin, ztail], axis=-1)
    ck_tab = jnp.concatenate([zeros, cos, cos, ztail], axis=-1)
    return c_tab, s_tab, ck_tab


def kernel(x, pre_norm_g, w_in, mla_q_norm_g, mla_w_uq, mla_kv_norm_g, mla_w_ukv, rel_bias, conv_dw_w, conv_dw_b, conv_ln_g, conv_ln_b, conv_pw_w, conv_pw_b, w_out, post_norm_g):
    batch, seq, d_model = x.shape
    depth = w_in.shape[0]
    assert d_model == D_MODEL and seq % ROW_TILE == 0 and ROW_TILE % TILE == 0
    n = batch * seq

    w_main, w_vt = _prep_in_proj_weights(w_in)
    wq, wqs, wkn, wvt_mla = _prep_mla_weights(mla_w_uq, mla_w_ukv)
    c_tab, s_tab, ck_tab = _rope_tables(seq)
    mla_scale = (HEAD_DIM + ROPE_DIM) ** -0.5 * LOG2E
    cq_tab, sq_tab = c_tab * mla_scale, s_tab * mla_scale
    gq = jnp.pad(mla_q_norm_g, ((0, 0), (0, 256 - Q_RANK)))
    dw = jnp.pad(conv_dw_w, ((0, 0), (0, CONV_HALO - CONV_WIDTH), (0, 0)))
    pw = conv_pw_w.astype(BF16)
    w_o = w_out.astype(BF16)
    bias = _bias_tiles(rel_bias)

    seq_major = lambda a: a.reshape(batch, seq, a.shape[-1])
    for l in range(depth):
        mla_params = (gq[l][None], mla_kv_norm_g[l][None], wq[l], wqs[l], wkn[l], wvt_mla[l],
                      cq_tab, sq_tab, ck_tab, s_tab)
        (qcat, kcat, mla_vt, sbq, sbk, mbq, mbqf, mbk, u_conv, u_gate, sbvt, mbvt, kmean) = _in_proj(
            x.reshape(n, D_MODEL), pre_norm_g[l][None], w_main[l], w_vt[l], mla_params, batch, seq)
        o_a = _mla_attn(qcat, kcat, mla_vt, batch, seq)
        o_b = _sb_attn(seq_major(sbq), seq_major(sbk), sbvt, batch, seq)
        o_c = _moba_attn(seq_major(mbq), seq_major(mbqf), seq_major(mbk), mbvt, kmean, bias, batch, seq)
        conv_params = (dw[l], conv_dw_b[l][None], conv_ln_g[l][None], conv_ln_b[l][None], pw[l],
                       conv_pw_b[l][None])
        x = _out_proj(seq_major(u_conv), conv_params, o_a, o_b, o_c, seq_major(u_gate), w_o[l],
                      post_norm_g[l][None], x)
    return x
```

```python
import functools
import math

import jax
import jax.numpy as jnp
from jax import lax
from jax.experimental import pallas as pl
from jax.experimental.pallas import tpu as pltpu

F32 = jnp.float32
BF16 = jnp.bfloat16

D_MODEL = 1024
HEAD_DIM = 64
N_HEADS = 4
GROUP = N_HEADS * HEAD_DIM
D_MIX = 4 * GROUP
Q_RANK = 192
KV_RANK = 128
ROPE_DIM = 32
ROPE_HALF = ROPE_DIM // 2
ROPE_THETA = 10000.0
MLA_HEAD_PAD = 128
CONV_WIDTH = 31
NUM_BUCKETS = 32
MAX_DISTANCE = 1024
MOBA_TOPK = 3
EPS = 1e-6

LANES = 128
TILE = 256
ROW_TILE = 512
CONV_HALO = 32
CONV_SHIFT_ROWS = CONV_HALO + ROW_TILE - 8
N_BIAS_TILES = 6
MLA_U = 384
MAIN_COLS = MLA_U + 5 * GROUP + 2 * GROUP + D_MIX
VMEM_LIMIT = 56 * 1024 * 1024
SB_EXP2_UNDERFLOW = -150.0
SIGN_BIT = 0x80000000
SCAN_RUN = TILE // 8
ONES_ROWS = 16
ACC_ROWS = HEAD_DIM + 8
LOG2E = math.log2(math.e)

_NT = (((1,), (1,)), ((), ()))


def _params(n_axes):
    return pltpu.CompilerParams(dimension_semantics=("arbitrary",) * n_axes,
                                vmem_limit_bytes=VMEM_LIMIT)


def _dot(a, b):
    return jnp.dot(a, b, preferred_element_type=F32)


def _dot_nt(a, b):
    return lax.dot_general(a, b, _NT, preferred_element_type=F32)


def _bias_tiles_kernel(rb_ref, o_ref):
    d = pl.program_id(0)
    row = lax.broadcasted_iota(jnp.int32, (TILE, TILE), 0)
    col = lax.broadcasted_iota(jnp.int32, (TILE, TILE), 1)
    n = jnp.maximum(d * TILE + col - row, 0)
    max_exact = NUM_BUCKETS // 2
    n_large = jnp.maximum(n, max_exact).astype(F32)
    large = max_exact + (jnp.log(n_large / max_exact) / math.log(MAX_DISTANCE / max_exact)
                         * (NUM_BUCKETS - max_exact)).astype(jnp.int32)
    large = jnp.minimum(large, NUM_BUCKETS - 1)
    bucket = jnp.where(n < max_exact, n, large)
    for h in range(N_HEADS):
        acc = jnp.zeros((TILE, TILE), F32)
        for b in range(NUM_BUCKETS):
            acc = jnp.where(bucket == b, rb_ref[b, h], acc)
        o_ref[0, h] = acc * LOG2E


def _bias_tiles(rel_bias):
    return pl.pallas_call(
        _bias_tiles_kernel,
        grid=(N_BIAS_TILES,),
        in_specs=[pl.BlockSpec(memory_space=pltpu.SMEM)],
        out_specs=pl.BlockSpec((1, N_HEADS, TILE, TILE), lambda d: (d, 0, 0, 0)),
        out_shape=jax.ShapeDtypeStruct((N_BIAS_TILES, N_HEADS, TILE, TILE), F32),
        compiler_params=_params(1),
        name="t5_bias_tiles",
    )(rel_bias)


def _in_proj_kernel(x_ref, g_ref, w_ref, wvt_ref, *rest):
    mla_params, rest = rest[:10], rest[10:]
    (mlaq_ref, mlak_ref, mlavt_ref, sbq_ref, sbk_ref, mbq_ref, mbqf_ref, mbk_ref, conv_ref, gate_ref,
     sbvt_ref, mbvt_ref, kmean_ref, kv_ref, kvscan_ref) = rest
    x = x_ref[...]
    ms = jnp.mean(x * x, axis=-1, keepdims=True)
    h = (x * lax.rsqrt(ms + EPS) * g_ref[...]).astype(BF16)

    def mm(lo, hi):
        return _dot(h, w_ref[:, lo:hi])

    c = 0
    _mla_prep(mm(c, c + MLA_U), *mla_params, mlaq_ref, mlak_ref, mlavt_ref); c += MLA_U
    sbq_ref[...] = (mm(c, c + GROUP) * LOG2E).astype(BF16); c += GROUP
    kv = mm(c, c + 2 * GROUP); c += 2 * GROUP
    n_chunks = 2 * GROUP // LANES
    for cc in range(n_chunks):
        kv_ref[cc] = kv[:, cc * LANES:(cc + 1) * LANES]
    for cc in range(n_chunks):
        for t in range(ROW_TILE // TILE):
            for i in range(SCAN_RUN):
                kvscan_ref[t * TILE + i * 8:t * TILE + (i + 1) * 8, cc * LANES:(cc + 1) * LANES] = (
                    kv_ref[cc, pl.ds(t * TILE + i, 8, stride=SCAN_RUN), :])
    sbk_ref[...] = kvscan_ref[:, :GROUP].astype(BF16)
    for t in range(ROW_TILE // TILE):
        sbvt_ref[0, t] = kvscan_ref[t * TILE:(t + 1) * TILE, GROUP:].T.astype(BF16)
    qf = mm(c, c + GROUP); c += GROUP
    mbqf_ref[...] = qf
    mbq_ref[...] = (qf * LOG2E).astype(BF16)
    kf = mm(c, c + GROUP); c += GROUP
    mbk_ref[...] = kf.astype(BF16)
    conv_ref[...] = mm(c, c + 2 * GROUP); c += 2 * GROUP
    gate = mm(c, c + D_MIX)
    gate_ref[...] = (gate * jax.nn.sigmoid(gate)).astype(BF16)
    mb_vt = _dot_nt(wvt_ref[...], h)
    for t in range(ROW_TILE // TILE):
        rows = slice(t * TILE, (t + 1) * TILE)
        kmean_ref[0, t] = jnp.mean(kf[rows], axis=0, keepdims=True)
        mbvt_ref[0, t] = mb_vt[:, rows].astype(BF16)


def _in_proj(x2, g, w_main, w_vt, mla_params, batch, seq):
    n = batch * seq
    nk = seq // TILE
    per_b = seq // ROW_TILE
    tpr = ROW_TILE // TILE
    row = lambda cols: pl.BlockSpec((ROW_TILE, cols), lambda i: (i, 0))
    full = lambda shp: pl.BlockSpec(shp, lambda i: (0,) * len(shp))
    tab = pl.BlockSpec((ROW_TILE, MLA_HEAD_PAD), lambda i: (i % per_b, 0))
    head_spec = pl.BlockSpec((1, N_HEADS, ROW_TILE, MLA_HEAD_PAD), lambda i: (i // per_b, 0, i % per_b, 0))
    vt_spec = pl.BlockSpec((1, tpr, GROUP, TILE), lambda i: (i // per_b, i % per_b, 0, 0))
    km_spec = pl.BlockSpec((1, tpr, 1, GROUP), lambda i: (i // per_b, i % per_b, 0, 0))
    sd = jax.ShapeDtypeStruct
    heads = sd((batch, N_HEADS, seq, MLA_HEAD_PAD), BF16)
    vt_tiles = sd((batch, nk, GROUP, TILE), BF16)
    return pl.pallas_call(
        _in_proj_kernel,
        grid=(n // ROW_TILE,),
        in_specs=[row(D_MODEL), full((1, D_MODEL)), full((D_MODEL, MAIN_COLS)),
                  full((GROUP, D_MODEL)),
                  full((1, 256)), full((1, KV_RANK)),
                  full((256, N_HEADS * MLA_HEAD_PAD)), full((256, N_HEADS * MLA_HEAD_PAD)),
                  full((KV_RANK, N_HEADS * MLA_HEAD_PAD)), full((GROUP, KV_RANK)),
                  tab, tab, tab, tab],
        out_specs=[head_spec, head_spec, vt_spec,
                   row(GROUP), row(GROUP), row(GROUP), row(GROUP), row(GROUP),
                   row(2 * GROUP), row(D_MIX), vt_spec, vt_spec, km_spec],
        out_shape=[heads, heads, vt_tiles,
                   sd((n, GROUP), BF16), sd((n, GROUP), BF16),
                   sd((n, GROUP), BF16), sd((n, GROUP), F32), sd((n, GROUP), BF16),
                   sd((n, 2 * GROUP), F32), sd((n, D_MIX), BF16),
                   vt_tiles, vt_tiles, sd((batch, nk, 1, GROUP), F32)],
        scratch_shapes=[pltpu.VMEM((2 * GROUP // LANES, ROW_TILE, LANES), F32),
                        pltpu.VMEM((ROW_TILE, 2 * GROUP), F32)],
        compiler_params=_params(1),
        name="in_proj",
    )(x2, g, w_main, w_vt, *mla_params)


def _mla_prep(u, gq_ref, gkv_ref, wq_ref, wqs_ref, wkn_ref, wvt_ref,
              cq_ref, sq_ref, ck_ref, sk_ref, q_ref, k_ref, vt_ref):
    cq = u[:, 0:256]
    lane = lax.broadcasted_iota(jnp.int32, cq.shape, 1)
    msq = jnp.sum(jnp.where(lane < Q_RANK, cq * cq, 0.0), axis=-1, keepdims=True) * (1.0 / Q_RANK)
    cqn = (cq * lax.rsqrt(msq + EPS) * gq_ref[...]).astype(BF16)
    qa = _dot(cqn, wq_ref[...])
    qs = _dot(cqn, wqs_ref[...])
    ckv = u[:, 256:384]
    msk = jnp.mean(ckv * ckv, axis=-1, keepdims=True)
    ckvn = (ckv * lax.rsqrt(msk + EPS) * gkv_ref[...]).astype(BF16)
    kn = _dot(ckvn, wkn_ref[...])
    tail = u[:, LANES:2 * LANES]
    kr = tail * ck_ref[...] + pltpu.roll(tail, LANES - ROPE_DIM, 1) * sk_ref[...]
    cq_t = cq_ref[...]
    sq_t = sq_ref[...]
    for h in range(N_HEADS):
        sl = slice(h * MLA_HEAD_PAD, (h + 1) * MLA_HEAD_PAD)
        q_ref[0, h] = (qa[:, sl] * cq_t + qs[:, sl] * sq_t).astype(BF16)
        k_ref[0, h] = (kn[:, sl] + kr).astype(BF16)
    vt = _dot_nt(wvt_ref[...], ckvn)
    for t in range(ROW_TILE // TILE):
        vt_ref[0, t] = vt[:, t * TILE:(t + 1) * TILE].astype(BF16)


def _head_rows(h):
    return slice(h * HEAD_DIM, (h + 1) * HEAD_DIM)


def _head_cols(h):
    return slice(h * TILE, (h + 1) * TILE)


def _key_rows(j):
    return pl.ds(pl.multiple_of(j * TILE, TILE), TILE)


def _store_head_masked_q(q, qbd_ref):
    lane = lax.broadcasted_iota(jnp.int32, (TILE, GROUP), 1)
    for h in range(N_HEADS):
        in_head = (lane >= h * HEAD_DIM) & (lane < (h + 1) * HEAD_DIM)
        qbd_ref[h * TILE:(h + 1) * TILE, :] = jnp.where(in_head, q, jnp.zeros_like(q))


def _sweep_scratch():
    s_buf = pltpu.VMEM((TILE, N_HEADS * TILE), F32)
    p_buf = pltpu.VMEM((N_HEADS, TILE, TILE), BF16)
    stat = pltpu.VMEM((N_HEADS, 1, TILE), F32)
    return [s_buf, s_buf, p_buf, p_buf, stat, stat, stat, pltpu.VMEM((N_HEADS, ACC_ROWS, TILE), F32)]


def _softmax_sweep(qi, score_fn, adjust_fn, vt_ref, o_ref, scratch):
    s_a, s_b, p_a, p_b, al_a, al_b, m_ref, acc_ref = scratch
    row = lax.broadcasted_iota(jnp.int32, (TILE, TILE), 0)
    col = lax.broadcasted_iota(jnp.int32, (TILE, TILE), 1)
    ones_rows = jnp.ones((ONES_ROWS, TILE), BF16)

    def key_tile(t):
        if isinstance(t, int) and t == 0:
            return qi
        return jnp.minimum(t - 1, jnp.maximum(qi - 1, 0))

    def softmax(h, t, s_ref, p_ref, al_ref):
        first = isinstance(t, int) and t == 0
        s, pen = adjust_fn(h, key_tile(t), s_ref[:, _head_cols(h)], first)
        if first:
            s = jnp.where(row <= col, s, -jnp.inf)
        s = s.astype(BF16)
        m_tile = jnp.max(s, axis=0, keepdims=True).astype(F32)
        if first:
            m_new = m_tile
            p = jnp.exp2(s - m_new.astype(BF16))
        else:
            dead = jnp.where(t <= qi, 0.0, -jnp.inf)
            pen = dead if pen is None else pen + dead
            m_old = m_ref[h]
            m_new = jnp.maximum(m_old, m_tile + pen)
            al_ref[h] = jnp.exp2(m_old - m_new)
            p = jnp.exp2(s - (m_new - pen).astype(BF16))
        m_ref[h] = m_new
        p_ref[h] = p

    def weighted_values(h, t, p_ref, al_ref):
        first = isinstance(t, int) and t == 0
        vt = jnp.concatenate([vt_ref[0, key_tile(t), _head_rows(h), :], ones_rows], axis=0)
        pv = _dot(vt, p_ref[h])[:ACC_ROWS]
        acc_ref[h] = pv if first else acc_ref[h] * al_ref[h] + pv

    def half(t, cur, nxt):
        (s_cur, p_cur, al_cur), (s_nxt, p_nxt, al_nxt) = cur, nxt
        for h in range(N_HEADS):
            score_fn(h, key_tile(t + 1), s_nxt)
            if not (isinstance(t, int) and t == 0):
                weighted_values(h, t - 1, p_nxt, al_nxt)
            softmax(h, t, s_cur, p_cur, al_cur)

    buf_a, buf_b = (s_a, p_a, al_a), (s_b, p_b, al_b)
    for h in range(N_HEADS):
        score_fn(h, key_tile(0), s_a)
    half(0, buf_a, buf_b)
    half(1, buf_b, buf_a)

    def pair(k, carry):
        half(2 * k, buf_a, buf_b)
        half(2 * k + 1, buf_b, buf_a)
        return carry

    def pairs_from(first, count):
        def body(k, carry):
            for i in range(count):
                pair(first + count * k + i, carry)
            return carry
        return body

    n_pairs = qi // 2 + 1
    n_quad = (n_pairs - 1) // 4
    n_double = ((n_pairs - 1) % 4) // 2
    lax.fori_loop(0, n_quad, pairs_from(1, 4), 0)
    lax.fori_loop(0, n_double, pairs_from(4 * n_quad + 1, 2), 0)
    lax.fori_loop(4 * n_quad + 2 * n_double + 1, n_pairs, pair, 0)
    for h in range(N_HEADS):
        weighted_values(h, 2 * n_pairs - 1, p_b, al_b)
    out_t = [acc_ref[h, :HEAD_DIM, :] / acc_ref[h, HEAD_DIM:HEAD_DIM + 1, :] for h in range(N_HEADS)]
    o_ref[0] = jnp.concatenate(out_t, axis=0).T.astype(o_ref.dtype)


def _mla_attn_kernel(q_ref, k_ref, vt_ref, o_ref, *scratch):
    qi = pl.program_id(1)

    def score_fn(h, j, s_ref):
        s_ref[:, _head_cols(h)] = _dot_nt(k_ref[0, h, _key_rows(j), :], q_ref[0, h])

    _softmax_sweep(qi, score_fn, lambda h, j, s, first: (s, None), vt_ref, o_ref, scratch)


def _mla_attn(q, k, vt, batch, seq):
    nk = seq // TILE
    return pl.pallas_call(
        _mla_attn_kernel,
        grid=(batch, nk),
        in_specs=[pl.BlockSpec((1, N_HEADS, TILE, MLA_HEAD_PAD), lambda b, i: (b, 0, i, 0)),
                  pl.BlockSpec((1, N_HEADS, seq, MLA_HEAD_PAD), lambda b, i: (b, 0, 0, 0)),
                  pl.BlockSpec((1, nk, GROUP, TILE), lambda b, i: (b, 0, 0, 0))],
        out_specs=pl.BlockSpec((1, TILE, GROUP), lambda b, i: (b, i, 0)),
        out_shape=jax.ShapeDtypeStruct((batch, seq, GROUP), BF16),
        scratch_shapes=_sweep_scratch(),
        compiler_params=_params(2),
        name="mla_attn",
    )(q, k, vt)


def _split_bf16(x):
    hi = x.astype(BF16)
    lo = (x - hi.astype(F32)).astype(BF16)
    return hi, lo


def _suffix_scan(x, carry):
    slabs = [x[i * 8:(i + 1) * 8, :] for i in range(SCAN_RUN)]
    rest_of_run = [None] * SCAN_RUN
    rest_of_run[SCAN_RUN - 1] = slabs[SCAN_RUN - 1]
    for i in range(SCAN_RUN - 2, -1, -1):
        rest_of_run[i] = rest_of_run[i + 1] + slabs[i]
    run_total = rest_of_run[0]
    sub = lax.broadcasted_iota(jnp.int32, run_total.shape, 0)
    from_here = run_total
    for d in (1, 2, 4):
        from_here = from_here + jnp.where(sub + d < 8, pltpu.roll(from_here, 8 - d, 0), 0.0)
    later_runs = from_here - run_total + carry
    suffix = jnp.concatenate([rest_of_run[i] + later_runs for i in range(SCAN_RUN)], axis=0)
    return suffix, from_here[0:1, :]


def _sb_attn_kernel(q_ref, k_ref, vt_ref, o_ref, qbd_ref, carry_ref, acc_ref):
    qi = pl.program_id(1)
    _store_head_masked_q(q_ref[0], qbd_ref)
    row = lax.broadcasted_iota(jnp.int32, (TILE, TILE), 0)
    col = lax.broadcasted_iota(jnp.int32, (TILE, TILE), 1)
    token = (row & 7) * SCAN_RUN + (row >> 3)
    past = token < col

    def tile(j, diag):
        z_all = _dot_nt(k_ref[0, _key_rows(j), :], qbd_ref[...])
        for h in range(N_HEADS):
            hs = _head_rows(h)
            z = z_all[:, h * TILE:(h + 1) * TILE]
            neg_abs = lax.bitcast_convert_type(
                lax.bitcast_convert_type(z, jnp.uint32) | jnp.uint32(SIGN_BIT), F32)
            w = jnp.maximum(z, 0.0) + jnp.log2(1.0 + jnp.exp2(neg_abs))
            if diag:
                w = jnp.where(past, w, 0.0)
                w_from_here, tile_sum = _suffix_scan(w, jnp.zeros((1, TILE), F32))
                a = jnp.where(past, jnp.exp2(z - w_from_here), 0.0)
                carry_ref[h] = tile_sum
                acc_ref[hs, :] = _dot(vt_ref[0, j, hs, :], a.astype(BF16))
            else:
                carry = carry_ref[h]
                w_from_here, tile_sum = _suffix_scan(w, carry)
                a = jnp.exp2(z - w_from_here)
                carry_ref[h] = carry + tile_sum
                acc_ref[hs, :] = acc_ref[hs, :] + _dot(vt_ref[0, j, hs, :], a.astype(BF16))

    def live():
        return jnp.min(carry_ref[...]) < -SB_EXP2_UNDERFLOW

    tile(qi, True)

    def body(c):
        tile(qi - 1 - c[0], False)
        return c[0] + 1, live()

    lax.while_loop(lambda c: (c[0] < qi) & c[1], body, (jnp.int32(0), live()))
    o_ref[0] = acc_ref[...].T.astype(o_ref.dtype)


def _sb_attn(q, k, vt, batch, seq):
    nk = seq // TILE
    return pl.pallas_call(
        _sb_attn_kernel,
        grid=(batch, nk),
        in_specs=[pl.BlockSpec((1, TILE, GROUP), lambda b, i: (b, i, 0)),
                  pl.BlockSpec((1, seq, GROUP), lambda b, i: (b, 0, 0)),
                  pl.BlockSpec((1, nk, GROUP, TILE), lambda b, i: (b, 0, 0, 0))],
        out_specs=pl.BlockSpec((1, TILE, GROUP), lambda b, i: (b, i, 0)),
        out_shape=jax.ShapeDtypeStruct((batch, seq, GROUP), BF16),
        scratch_shapes=[pltpu.VMEM((N_HEADS * TILE, GROUP), BF16), pltpu.VMEM((N_HEADS, 1, TILE), F32),
                        pltpu.VMEM((GROUP, TILE), F32)],
        compiler_params=_params(2),
        name="sb_attn",
    )(q, k, vt)


def _moba_attn_kernel(nk, q_ref, qf_ref, k_ref, vt_ref, km_ref, bias_ref, o_ref,
                      qbd_ref, sel_ref, *scratch):
    qi = pl.program_id(1)
    _store_head_masked_q(q_ref[0], qbd_ref)
    km = km_ref[0, :, 0, :]
    lane = lax.broadcasted_iota(jnp.int32, (nk, GROUP), 1)
    km_hi, km_lo = _split_bf16(jnp.concatenate(
        [jnp.where((lane >= h * HEAD_DIM) & (lane < (h + 1) * HEAD_DIM), km, 0.0) for h in range(N_HEADS)], axis=0))
    qf_hi, qf_lo = _split_bf16(qf_ref[0])
    gate_all = _dot_nt(km_hi, qf_hi) + (_dot_nt(km_hi, qf_lo) + _dot_nt(km_lo, qf_hi))
    blk = lax.broadcasted_iota(jnp.int32, (nk, TILE), 0)
    for h in range(N_HEADS):
        gate = gate_all[h * nk:(h + 1) * nk, :]
        rival = jnp.where(blk < qi, gate, -jnp.inf)
        beaten = jnp.zeros((nk, TILE), jnp.int32)
        for jp in range(nk):
            gj = rival[jp:jp + 1, :]
            beaten = beaten + jnp.where((gj > gate) | ((gj == gate) & (jp < blk)), 1, 0)
        keep = (blk < qi) & (beaten < MOBA_TOPK)
        sel_ref[h] = jnp.where(keep, 0.0, -jnp.inf)

    def score_fn(h, j, s_ref):
        s_ref[:, _head_cols(h)] = _dot_nt(k_ref[0, _key_rows(j), :], qbd_ref[_head_cols(h), :])

    def adjust_fn(h, j, s, first):
        if first:
            return s + bias_ref[0, h], None
        d = jnp.minimum(qi - j, N_BIAS_TILES - 1)
        return s + bias_ref[d, h], sel_ref[h, pl.ds(j, 1), :]

    _softmax_sweep(qi, score_fn, adjust_fn, vt_ref, o_ref, scratch)


def _moba_attn(q, qf, k, vt, kmean, bias, batch, seq):
    nk = seq // TILE
    return pl.pallas_call(
        functools.partial(_moba_attn_kernel, nk),
        grid=(batch, nk),
        in_specs=[pl.BlockSpec((1, TILE, GROUP), lambda b, i: (b, i, 0)),
                  pl.BlockSpec((1, TILE, GROUP), lambda b, i: (b, i, 0)),
                  pl.BlockSpec((1, seq, GROUP), lambda b, i: (b, 0, 0)),
                  pl.BlockSpec((1, nk, GROUP, TILE), lambda b, i: (b, 0, 0, 0)),
                  pl.BlockSpec((1, nk, 1, GROUP), lambda b, i: (b, 0, 0, 0)),
                  pl.BlockSpec((N_BIAS_TILES, N_HEADS, TILE, TILE), lambda b, i: (0, 0, 0, 0))],
        out_specs=pl.BlockSpec((1, TILE, GROUP), lambda b, i: (b, i, 0)),
        out_shape=jax.ShapeDtypeStruct((batch, seq, GROUP), BF16),
        scratch_shapes=[pltpu.VMEM((N_HEADS * TILE, GROUP), BF16), pltpu.VMEM((N_HEADS, nk, TILE), F32)]
        + _sweep_scratch(),
        compiler_params=_params(2),
        name="moba_attn",
    )(q, qf, k, vt, kmean, bias)


def _conv_module(i, u_ref, halo_ref, dw_ref, dwb_ref, lng_ref, lnb_ref, pw_ref, pwb_ref, ext_ref, shift_ref):
    um = u_ref[0]
    uh = halo_ref[0]
    xh = uh[:, :GROUP] * jax.nn.sigmoid(uh[:, GROUP:])
    ext_ref[0:CONV_HALO, :] = jnp.where(i == 0, 0.0, xh)
    ext_ref[CONV_HALO:, :] = um[:, :GROUP] * jax.nn.sigmoid(um[:, GROUP:])
    base = CONV_HALO - (CONV_WIDTH - 1)
    for b in range(1, 8):
        shift_ref[b - 1] = ext_ref[b:b + CONV_SHIFT_ROWS, :]
    y = jnp.zeros((ROW_TILE, GROUP), F32)
    for w in range(CONV_WIDTH):
        a, b = divmod(base + w, 8)
        rows = slice(8 * a, 8 * a + ROW_TILE)
        tap = ext_ref[rows, :] if b == 0 else shift_ref[b - 1, rows, :]
        y = y + tap * dw_ref[w:w + 1, :]
    y = y + dwb_ref[...]
    mu = jnp.mean(y, axis=-1, keepdims=True)
    yc = y - mu
    var = jnp.mean(yc * yc, axis=-1, keepdims=True)
    yn = yc * lax.rsqrt(var + EPS) * lng_ref[...] + lnb_ref[...]
    sw = yn * jax.nn.sigmoid(yn)
    return _dot(sw.astype(BF16), pw_ref[...]) + pwb_ref[...]


def _out_proj_kernel(u_ref, halo_ref, dw_ref, dwb_ref, lng_ref, lnb_ref, pw_ref, pwb_ref,
                     oa_ref, ob_ref, oc_ref, gate_ref, w_ref, g_ref, x_ref, o_ref, ext_ref, shift_ref):
    od = _conv_module(pl.program_id(1), u_ref, halo_ref, dw_ref, dwb_ref, lng_ref, lnb_ref, pw_ref, pwb_ref,
                      ext_ref, shift_ref)
    sg = gate_ref[0].astype(F32)
    mix =jnp.concatenate([oa_ref[0].astype(F32), ob_ref[0].astype(F32), oc_ref[0].astype(F32), od], axis=-1)
    y = _dot((mix * sg).astype(BF16), w_ref[...])
    ms = jnp.mean(y * y, axis=-1, keepdims=True)
    o_ref[0] = x_ref[0] + y * lax.rsqrt(ms + EPS) * g_ref[...]


def _out_proj(u_conv, conv_params, oa, ob, oc, gate, w, g, x3):
    batch, seq, _ = x3.shape
    per_b = seq // ROW_TILE
    halo_per_tile = ROW_TILE // CONV_HALO
    row = lambda cols: pl.BlockSpec((1, ROW_TILE, cols), lambda b, i: (b, i, 0))
    full = lambda shp: pl.BlockSpec(shp, lambda b, i: (0,) * len(shp))
    return pl.pallas_call(
        _out_proj_kernel,
        grid=(batch, per_b),
        in_specs=[row(2 * GROUP),
                  pl.BlockSpec((1, CONV_HALO, 2 * GROUP),
                               lambda b, i: (b, jnp.maximum(i * halo_per_tile - 1, 0), 0)),
                  full((CONV_HALO, GROUP)), full((1, GROUP)), full((1, GROUP)), full((1, GROUP)),
                  full((GROUP, GROUP)), full((1, GROUP)),
                  row(GROUP), row(GROUP), row(GROUP), row(D_MIX),
                  full((D_MIX, D_MODEL)), full((1, D_MODEL)), row(D_MODEL)],
        out_specs=row(D_MODEL),
        out_shape=jax.ShapeDtypeStruct((batch, seq, D_MODEL), F32),
        scratch_shapes=[pltpu.VMEM((CONV_HALO + ROW_TILE, GROUP), F32),
                        pltpu.VMEM((7, CONV_SHIFT_ROWS, GROUP), F32)],
        compiler_params=_params(2),
        name="out_proj",
    )(u_conv, u_conv, *conv_params, oa, ob, oc, gate, w, g, x3)


def _prep_in_proj_weights(w_in):
    c = 0
    cq = w_in[..., c:c + Q_RANK]; c += Q_RANK
    ckv = w_in[..., c:c + KV_RANK]; c += KV_RANK
    kr = w_in[..., c:c + ROPE_DIM]; c += ROPE_DIM
    sbq, sbk, sbv = (w_in[..., c + i * GROUP:c + (i + 1) * GROUP] for i in range(3)); c += 3 * GROUP
    mbq, mbk, mbv = (w_in[..., c + i * GROUP:c + (i + 1) * GROUP] for i in range(3)); c += 3 * GROUP
    conv = w_in[..., c:c + 2 * GROUP]; c += 2 * GROUP
    gate = w_in[..., c:c + D_MIX]
    kr_swapped = jnp.concatenate([kr[..., ROPE_HALF:], kr[..., :ROPE_HALF]], axis=-1)
    scale = HEAD_DIM ** -0.5
    main = jnp.concatenate([cq, kr, kr_swapped, ckv, sbq * scale, sbk, sbv, mbq * scale, mbk, conv, gate], axis=-1)
    return main.astype(BF16), jnp.swapaxes(mbv, 1, 2).astype(BF16)


def _prep_mla_weights(w_uq, w_ukv):
    depth = w_uq.shape[0]
    qh = w_uq.reshape(depth, Q_RANK, N_HEADS, HEAD_DIM + ROPE_DIM)
    nope, r1, r2 = qh[..., :HEAD_DIM], qh[..., HEAD_DIM:HEAD_DIM + ROPE_HALF], qh[..., HEAD_DIM + ROPE_HALF:]
    zpad = jnp.zeros((depth, Q_RANK, N_HEADS, MLA_HEAD_PAD - HEAD_DIM - ROPE_DIM), w_uq.dtype)
    row_pad = ((0, 0), (0, 256 - Q_RANK), (0, 0))
    wq = jnp.pad(jnp.concatenate([nope, r1, r2, zpad], -1).reshape(depth, Q_RANK, -1), row_pad)
    wqs = jnp.pad(jnp.concatenate([jnp.zeros_like(nope), r2, r1, zpad], -1).reshape(depth, Q_RANK, -1), row_pad)
    kvh = w_ukv.reshape(depth, KV_RANK, N_HEADS, 2 * HEAD_DIM)
    k_nope, v = kvh[..., :HEAD_DIM], kvh[..., HEAD_DIM:]
    wkn = jnp.concatenate([k_nope, jnp.zeros_like(k_nope)], -1).reshape(depth, KV_RANK, -1)
    wvt = jnp.swapaxes(v.reshape(depth, KV_RANK, GROUP), 1, 2)
    return wq.astype(BF16), wqs.astype(BF16), wkn.astype(BF16), wvt.astype(BF16)


def _rope_tables(seq):
    freqs = ROPE_THETA ** (-jnp.arange(ROPE_HALF, dtype=F32) / ROPE_HALF)
    ang = jnp.arange(seq, dtype=jnp.int32).astype(F32)[:, None] * freqs[None, :]
    cos, sin = jnp.cos(ang), jnp.sin(ang)
    ones, zeros = jnp.ones((seq, HEAD_DIM), F32), jnp.zeros((seq, HEAD_DIM), F32)
    ztail = jnp.zeros((seq, MLA_HEAD_PAD - HEAD_DIM - ROPE_DIM), F32)
    c_tab = jnp.concatenate([ones, cos, cos, ztail], axis=-1)
    s_tab = jnp.concatenate([zeros, -sin, sin, ztail], axis=-1)
    ck_tab = jnp.concatenate([zeros, cos, cos, ztail], axis=-1)
    return c_tab, s_tab, ck_tab


def kernel(x, pre_norm_g, w_in, mla_q_norm_g, mla_w_uq, mla_kv_norm_g, mla_w_ukv, rel_bias, conv_dw_w, conv_dw_b, conv_ln_g, conv_ln_b, conv_pw_w, conv_pw_b, w_out, post_norm_g):
    batch, seq, d_model = x.shape
    depth = w_in.shape[0]
    assert d_model == D_MODEL and seq % ROW_TILE == 0 and ROW_TILE % TILE == 0
    n = batch * seq

    w_main, w_vt = _prep_in_proj_weights(w_in)
    wq, wqs, wkn, wvt_mla = _prep_mla_weights(mla_w_uq, mla_w_ukv)
    c_tab, s_tab, ck_tab = _rope_tables(seq)
    mla_scale = (HEAD_DIM + ROPE_DIM) ** -0.5 * LOG2E
    cq_tab, sq_tab = c_tab * mla_scale, s_tab * mla_scale
    gq = jnp.pad(mla_q_norm_g, ((0, 0), (0, 256 - Q_RANK)))
    dw = jnp.pad(conv_dw_w, ((0, 0), (0, CONV_HALO - CONV_WIDTH), (0, 0)))
    pw = conv_pw_w.astype(BF16)
    w_o = w_out.astype(BF16)
    bias = _bias_tiles(rel_bias)

    seq_major = lambda a: a.reshape(batch, seq, a.shape[-1])
    for l in range(depth):
        mla_params = (gq[l][None], mla_kv_norm_g[l][None], wq[l], wqs[l], wkn[l], wvt_mla[l],
                      cq_tab, sq_tab, ck_tab, s_tab)
        (qcat, kcat, mla_vt, sbq, sbk, mbq, mbqf, mbk, u_conv, u_gate, sbvt, mbvt, kmean) = _in_proj(
            x.reshape(n, D_MODEL), pre_norm_g[l][None], w_main[l], w_vt[l], mla_params, batch, seq)
        o_a = _mla_attn(qcat, kcat, mla_vt, batch, seq)
        o_b = _sb_attn(seq_major(sbq), seq_major(sbk), sbvt, batch, seq)
        o_c = _moba_attn(seq_major(mbq), seq_major(mbqf), seq_major(mbk), mbvt, kmean, bias, batch, seq)
        conv_params = (dw[l], conv_dw_b[l][None], conv_ln_g[l][None], conv_ln_b[l][None], pw[l],
                       conv_pw_b[l][None])
        x = _out_proj(seq_major(u_conv), conv_params, o_a, o_b, o_c, seq_major(u_gate), w_o[l],
                      post_norm_g[l][None], x)
    return x
```

```python
import functools
import math

import jax
import jax.numpy as jnp
from jax import lax
from jax.experimental import pallas as pl
from jax.experimental.pallas import tpu as pltpu

F32 = jnp.float32
BF16 = jnp.bfloat16

D_MODEL = 1024
HEAD_DIM = 64
N_HEADS = 4
GROUP = N_HEADS * HEAD_DIM
D_MIX = 4 * GROUP
Q_RANK = 192
KV_RANK = 128
ROPE_DIM = 32
ROPE_HALF = ROPE_DIM // 2
ROPE_THETA = 10000.0
MLA_HEAD_PAD = 128
CONV_WIDTH = 31
NUM_BUCKETS = 32
MAX_DISTANCE = 1024
MOBA_TOPK = 3
EPS = 1e-6

LANES = 128
TILE = 256
ROW_TILE = 1024
CONV_HALO = 32
CONV_SHIFT_ROWS = CONV_HALO + ROW_TILE - 8
N_BIAS_TILES = 6
MLA_U = 384
MAIN_COLS = MLA_U + 5 * GROUP + 2 * GROUP + D_MIX
VMEM_LIMIT = 56 * 1024 * 1024
SB_EXP2_UNDERFLOW = -150.0
SIGN_BIT = 0x80000000
SCAN_RUN = TILE // 8
ONES_ROWS = 16
ACC_ROWS = HEAD_DIM + 8
LOG2E = math.log2(math.e)

_NT = (((1,), (1,)), ((), ()))


def _params(n_axes):
    return pltpu.CompilerParams(dimension_semantics=("arbitrary",) * n_axes,
                                vmem_limit_bytes=VMEM_LIMIT)


def _dot(a, b):
    return jnp.dot(a, b, preferred_element_type=F32)


def _dot_nt(a, b):
    return lax.dot_general(a, b, _NT, preferred_element_type=F32)


def _bias_tiles_kernel(rb_ref, o_ref):
    d = pl.program_id(0)
    row = lax.broadcasted_iota(jnp.int32, (TILE, TILE), 0)
    col = lax.broadcasted_iota(jnp.int32, (TILE, TILE), 1)
    n = jnp.maximum(d * TILE + col - row, 0)
    max_exact = NUM_BUCKETS // 2
    n_large = jnp.maximum(n, max_exact).astype(F32)
    large = max_exact + (jnp.log(n_large / max_exact) / math.log(MAX_DISTANCE / max_exact)
                         * (NUM_BUCKETS - max_exact)).astype(jnp.int32)
    large = jnp.minimum(large, NUM_BUCKETS - 1)
    bucket = jnp.where(n < max_exact, n, large)
    for h in range(N_HEADS):
        acc = jnp.zeros((TILE, TILE), F32)
        for b in range(NUM_BUCKETS):
            acc = jnp.where(bucket == b, rb_ref[b, h], acc)
        o_ref[0, h] = acc * LOG2E


def _bias_tiles(rel_bias):
    return pl.pallas_call(
        _bias_tiles_kernel,
        grid=(N_BIAS_TILES,),
        in_specs=[pl.BlockSpec(memory_space=pltpu.SMEM)],
        out_specs=pl.BlockSpec((1, N_HEADS, TILE, TILE), lambda d: (d, 0, 0, 0)),
        out_shape=jax.ShapeDtypeStruct((N_BIAS_TILES, N_HEADS, TILE, TILE), F32),
        compiler_params=_params(1),
        name="t5_bias_tiles",
    )(rel_bias)


def _in_proj_kernel(x_ref, g_ref, w_ref, wvt_ref, *rest):
    mla_params, rest = rest[:10], rest[10:]
    (mlaq_ref, mlak_ref, mlavt_ref, sbq_ref, sbk_ref, mbq_ref, mbqf_ref, mbk_ref, conv_ref, gate_ref,
     sbvt_ref, mbvt_ref, kmean_ref, kv_ref, kvscan_ref) = rest
    x = x_ref[...]
    ms = jnp.mean(x * x, axis=-1, keepdims=True)
    h = (x * lax.rsqrt(ms + EPS) * g_ref[...]).astype(BF16)

    def mm(lo, hi):
        return _dot(h, w_ref[:, lo:hi])

    c = 0
    _mla_prep(mm(c, c + MLA_U), *mla_params, mlaq_ref, mlak_ref, mlavt_ref); c += MLA_U
    sbq_ref[...] = (mm(c, c + GROUP) * LOG2E).astype(BF16); c += GROUP
    kv = mm(c, c + 2 * GROUP); c += 2 * GROUP
    n_chunks = 2 * GROUP // LANES
    for cc in range(n_chunks):
        kv_ref[cc] = kv[:, cc * LANES:(cc + 1) * LANES]
    for cc in range(n_chunks):
        for t in range(ROW_TILE // TILE):
            for i in range(SCAN_RUN):
                kvscan_ref[t * TILE + i * 8:t * TILE + (i + 1) * 8, cc * LANES:(cc + 1) * LANES] = (
                    kv_ref[cc, pl.ds(t * TILE + i, 8, stride=SCAN_RUN), :])
    sbk_ref[...] = kvscan_ref[:, :GROUP].astype(BF16)
    for t in range(ROW_TILE // TILE):
        sbvt_ref[0, t] = kvscan_ref[t * TILE:(t + 1) * TILE, GROUP:].T.astype(BF16)
    qf = mm(c, c + GROUP); c += GROUP
    mbqf_ref[...] = qf
    mbq_ref[...] = (qf * LOG2E).astype(BF16)
    kf = mm(c, c + GROUP); c += GROUP
    mbk_ref[...] = kf.astype(BF16)
    conv_ref[...] = mm(c, c + 2 * GROUP); c += 2 * GROUP
    gate = mm(c, c + D_MIX)
    gate_ref[...] = (gate * jax.nn.sigmoid(gate)).astype(BF16)
    mb_vt = _dot_nt(wvt_ref[...], h)
    for t in range(ROW_TILE // TILE):
        rows = slice(t * TILE, (t + 1) * TILE)
        kmean_ref[0, t] = jnp.mean(kf[rows], axis=0, keepdims=True)
        mbvt_ref[0, t] = mb_vt[:, rows].astype(BF16)


def _in_proj(x2, g, w_main, w_vt, mla_params, batch, seq):
    n = batch * seq
    nk = seq // TILE
    per_b = seq // ROW_TILE
    tpr = ROW_TILE // TILE
    row = lambda cols: pl.BlockSpec((ROW_TILE, cols), lambda i: (i, 0))
    full = lambda shp: pl.BlockSpec(shp, lambda i: (0,) * len(shp))
    tab = pl.BlockSpec((ROW_TILE, MLA_HEAD_PAD), lambda i: (i % per_b, 0))
    head_spec = pl.BlockSpec((1, N_HEADS, ROW_TILE, MLA_HEAD_PAD), lambda i: (i // per_b, 0, i % per_b, 0))
    vt_spec = pl.BlockSpec((1, tpr, GROUP, TILE), lambda i: (i // per_b, i % per_b, 0, 0))
    km_spec = pl.BlockSpec((1, tpr, 1, GROUP), lambda i: (i // per_b, i % per_b, 0, 0))
    sd = jax.ShapeDtypeStruct
    heads = sd((batch, N_HEADS, seq, MLA_HEAD_PAD), BF16)
    vt_tiles = sd((batch, nk, GROUP, TILE), BF16)
    return pl.pallas_call(
        _in_proj_kernel,
        grid=(n // ROW_TILE,),
        in_specs=[row(D_MODEL), full((1, D_MODEL)), full((D_MODEL, MAIN_COLS)),
                  full((GROUP, D_MODEL)),
                  full((1, 256)), full((1, KV_RANK)),
                  full((256, N_HEADS * MLA_HEAD_PAD)), full((256, N_HEADS * MLA_HEAD_PAD)),
                  full((KV_RANK, N_HEADS * MLA_HEAD_PAD)), full((GROUP, KV_RANK)),
                  tab, tab, tab, tab],
        out_specs=[head_spec, head_spec, vt_spec,
                   row(GROUP), row(GROUP), row(GROUP), row(GROUP), row(GROUP),
                   row(2 * GROUP), row(D_MIX), vt_spec, vt_spec, km_spec],
        out_shape=[heads, heads, vt_tiles,
                   sd((n, GROUP), BF16), sd((n, GROUP), BF16),
                   sd((n, GROUP), BF16), sd((n, GROUP), F32), sd((n, GROUP), BF16),
                   sd((n, 2 * GROUP), F32), sd((n, D_MIX), BF16),
                   vt_tiles, vt_tiles, sd((batch, nk, 1, GROUP), F32)],
        scratch_shapes=[pltpu.VMEM((2 * GROUP // LANES, ROW_TILE, LANES), F32),
                        pltpu.VMEM((ROW_TILE, 2 * GROUP), F32)],
        compiler_params=_params(1),
        name="in_proj",
    )(x2, g, w_main, w_vt, *mla_params)


def _mla_prep(u, gq_ref, gkv_ref, wq_ref, wqs_ref, wkn_ref, wvt_ref,
              cq_ref, sq_ref, ck_ref, sk_ref, q_ref, k_ref, vt_ref):
    cq = u[:, 0:256]
    lane = lax.broadcasted_iota(jnp.int32, cq.shape, 1)
    msq = jnp.sum(jnp.where(lane < Q_RANK, cq * cq, 0.0), axis=-1, keepdims=True) * (1.0 / Q_RANK)
    cqn = (cq * lax.rsqrt(msq + EPS) * gq_ref[...]).astype(BF16)
    qa = _dot(cqn, wq_ref[...])
    qs = _dot(cqn, wqs_ref[...])
    ckv = u[:, 256:384]
    msk = jnp.mean(ckv * ckv, axis=-1, keepdims=True)
    ckvn = (ckv * lax.rsqrt(msk + EPS) * gkv_ref[...]).astype(BF16)
    kn = _dot(ckvn, wkn_ref[...])
    tail = u[:, LANES:2 * LANES]
    kr = tail * ck_ref[...] + pltpu.roll(tail, LANES - ROPE_DIM, 1) * sk_ref[...]
    cq_t = cq_ref[...]
    sq_t = sq_ref[...]
    for h in range(N_HEADS):
        sl = slice(h * MLA_HEAD_PAD, (h + 1) * MLA_HEAD_PAD)
        q_ref[0, h] = (qa[:, sl] * cq_t + qs[:, sl] * sq_t).astype(BF16)
        k_ref[0, h] = (kn[:, sl] + kr).astype(BF16)
    vt = _dot_nt(wvt_ref[...], ckvn)
    for t in range(ROW_TILE // TILE):
        vt_ref[0, t] = vt[:, t * TILE:(t + 1) * TILE].astype(BF16)


def _head_rows(h):
    return slice(h * HEAD_DIM, (h + 1) * HEAD_DIM)


def _head_cols(h):
    return slice(h * TILE, (h + 1) * TILE)


def _key_rows(j):
    return pl.ds(pl.multiple_of(j * TILE, TILE), TILE)


def _store_head_masked_q(q, qbd_ref):
    lane = lax.broadcasted_iota(jnp.int32, (TILE, GROUP), 1)
    for h in range(N_HEADS):
        in_head = (lane >= h * HEAD_DIM) & (lane < (h + 1) * HEAD_DIM)
        qbd_ref[h * TILE:(h + 1) * TILE, :] = jnp.where(in_head, q, jnp.zeros_like(q))


def _sweep_scratch():
    s_buf = pltpu.VMEM((TILE, N_HEADS * TILE), F32)
    p_buf = pltpu.VMEM((N_HEADS, TILE, TILE), BF16)
    stat = pltpu.VMEM((N_HEADS, 1, TILE), F32)
    return [s_buf, s_buf, p_buf, p_buf, stat, stat, stat, pltpu.VMEM((N_HEADS, ACC_ROWS, TILE), F32)]


def _softmax_sweep(qi, score_fn, adjust_fn, vt_ref, o_ref, scratch):
    s_a, s_b, p_a, p_b, al_a, al_b, m_ref, acc_ref = scratch
    row = lax.broadcasted_iota(jnp.int32, (TILE, TILE), 0)
    col = lax.broadcasted_iota(jnp.int32, (TILE, TILE), 1)
    ones_rows = jnp.ones((ONES_ROWS, TILE), BF16)

    def key_tile(t):
        if isinstance(t, int) and t == 0:
            return qi
        return jnp.minimum(t - 1, jnp.maximum(qi - 1, 0))

    def softmax(h, t, s_ref, p_ref, al_ref):
        first = isinstance(t, int) and t == 0
        s, pen = adjust_fn(h, key_tile(t), s_ref[:, _head_cols(h)], first)
        if first:
            s = jnp.where(row <= col, s, -jnp.inf)
        s = s.astype(BF16)
        m_tile = jnp.max(s, axis=0, keepdims=True).astype(F32)
        if first:
            m_new = m_tile
            p = jnp.exp2(s - m_new.astype(BF16))
        else:
            dead = jnp.where(t <= qi, 0.0, -jnp.inf)
            pen = dead if pen is None else pen + dead
            m_old = m_ref[h]
            m_new = jnp.maximum(m_old, m_tile + pen)
            al_ref[h] = jnp.exp2(m_old - m_new)
            p = jnp.exp2(s - (m_new - pen).astype(BF16))
        m_ref[h] = m_new
        p_ref[h] = p

    def weighted_values(h, t, p_ref, al_ref):
        first = isinstance(t, int) and t == 0
        vt = jnp.concatenate([vt_ref[0, key_tile(t), _head_rows(h), :], ones_rows], axis=0)
        pv = _dot(vt, p_ref[h])[:ACC_ROWS]
        acc_ref[h] = pv if first else acc_ref[h] * al_ref[h] + pv

    def half(t, cur, nxt):
        (s_cur, p_cur, al_cur), (s_nxt, p_nxt, al_nxt) = cur, nxt
        for h in range(N_HEADS):
            score_fn(h, key_tile(t + 1), s_nxt)
            if not (isinstance(t, int) and t == 0):
                weighted_values(h, t - 1, p_nxt, al_nxt)
            softmax(h, t, s_cur, p_cur, al_cur)

    buf_a, buf_b = (s_a, p_a, al_a), (s_b, p_b, al_b)
    for h in range(N_HEADS):
        score_fn(h, key_tile(0), s_a)
    half(0, buf_a, buf_b)
    half(1, buf_b, buf_a)

    def pair(k, carry):
        half(2 * k, buf_a, buf_b)
        half(2 * k + 1, buf_b, buf_a)
        return carry

    def pairs_from(first, count):
        def body(k, carry):
            for i in range(count):
                pair(first + count * k + i, carry)
            return carry
        return body

    n_pairs = qi // 2 + 1
    n_quad = (n_pairs - 1) // 4
    n_double = ((n_pairs - 1) % 4) // 2
    lax.fori_loop(0, n_quad, pairs_from(1, 4), 0)
    lax.fori_loop(0, n_double, pairs_from(4 * n_quad + 1, 2), 0)
    lax.fori_loop(4 * n_quad + 2 * n_double + 1, n_pairs, pair, 0)
    for h in range(N_HEADS):
        weighted_values(h, 2 * n_pairs - 1, p_b, al_b)
    out_t = [acc_ref[h, :HEAD_DIM, :] / acc_ref[h, HEAD_DIM:HEAD_DIM + 1, :] for h in range(N_HEADS)]
    o_ref[0] = jnp.concatenate(out_t, axis=0).T.astype(o_ref.dtype)


def _mla_attn_kernel(q_ref, k_ref, vt_ref, o_ref, *scratch):
    qi = pl.program_id(1)

    def score_fn(h, j, s_ref):
        s_ref[:, _head_cols(h)] = _dot_nt(k_ref[0, h, _key_rows(j), :], q_ref[0, h])

    _softmax_sweep(qi, score_fn, lambda h, j, s, first: (s, None), vt_ref, o_ref, scratch)


def _mla_attn(q, k, vt, batch, seq):
    nk = seq // TILE
    return pl.pallas_call(
        _mla_attn_kernel,
        grid=(batch, nk),
        in_specs=[pl.BlockSpec((1, N_HEADS, TILE, MLA_HEAD_PAD), lambda b, i: (b, 0, i, 0)),
                  pl.BlockSpec((1, N_HEADS, seq, MLA_HEAD_PAD), lambda b, i: (b, 0, 0, 0)),
                  pl.BlockSpec((1, nk, GROUP, TILE), lambda b, i: (b, 0, 0, 0))],
        out_specs=pl.BlockSpec((1, TILE, GROUP), lambda b, i: (b, i, 0)),
        out_shape=jax.ShapeDtypeStruct((batch, seq, GROUP), BF16),
        scratch_shapes=_sweep_scratch(),
        compiler_params=_params(2),
        name="mla_attn",
    )(q, k, vt)


def _split_bf16(x):
    hi = x.astype(BF16)
    lo = (x - hi.astype(F32)).astype(BF16)
    return hi, lo


def _suffix_scan(x, carry):
    slabs = [x[i * 8:(i + 1) * 8, :] for i in range(SCAN_RUN)]
    rest_of_run = [None] * SCAN_RUN
    rest_of_run[SCAN_RUN - 1] = slabs[SCAN_RUN - 1]
    for i in range(SCAN_RUN - 2, -1, -1):
        rest_of_run[i] = rest_of_run[i + 1] + slabs[i]
    run_total = rest_of_run[0]
    sub = lax.broadcasted_iota(jnp.int32, run_total.shape, 0)
    from_here = run_total
    for d in (1, 2, 4):
        from_here = from_here + jnp.where(sub + d < 8, pltpu.roll(from_here, 8 - d, 0), 0.0)
    later_runs = from_here - run_total + carry
    suffix = jnp.concatenate([rest_of_run[i] + later_runs for i in range(SCAN_RUN)], axis=0)
    return suffix, from_here[0:1, :]


def _sb_attn_kernel(q_ref, k_ref, vt_ref, o_ref, qbd_ref, carry_ref, acc_ref):
    qi = pl.program_id(1)
    _store_head_masked_q(q_ref[0], qbd_ref)
    row = lax.broadcasted_iota(jnp.int32, (TILE, TILE), 0)
    col = lax.broadcasted_iota(jnp.int32, (TILE, TILE), 1)
    token = (row & 7) * SCAN_RUN + (row >> 3)
    past = token < col

    def scores(j):
        return _dot_nt(k_ref[0, _key_rows(j), :], qbd_ref[...])

    def finish(j, z_all, diag):
        for h in range(N_HEADS):
            hs = _head_rows(h)
            z = z_all[:, h * TILE:(h + 1) * TILE]
            neg_abs = lax.bitcast_convert_type(
                lax.bitcast_convert_type(z, jnp.uint32) | jnp.uint32(SIGN_BIT), F32)
            w = jnp.maximum(z, 0.0) + jnp.log2(1.0 + jnp.exp2(neg_abs))
            if diag:
                w = jnp.where(past, w, 0.0)
                w_from_here, tile_sum = _suffix_scan(w, jnp.zeros((1, TILE), F32))
                a = jnp.where(past, jnp.exp2(z - w_from_here), 0.0)
                carry_ref[h] = tile_sum
                acc_ref[hs, :] = _dot(vt_ref[0, j, hs, :], a.astype(BF16))
            else:
                carry = carry_ref[h]
                w_from_here, tile_sum = _suffix_scan(w, carry)
                a = jnp.exp2(z - w_from_here)
                carry_ref[h] = carry + tile_sum
                acc_ref[hs, :] = acc_ref[hs, :] + _dot(vt_ref[0, j, hs, :], a.astype(BF16))

    def live():
        return jnp.min(carry_ref[...]) < -SB_EXP2_UNDERFLOW

    @pl.when(qi == 0)
    def _():
        finish(qi, scores(qi), True)

    @pl.when(qi > 0)
    def _():
        z_diag, z_prev = scores(qi), scores(qi - 1)
        finish(qi, z_diag, True)
        finish(qi - 1, z_prev, False)

    def body(c):
        j = qi - 1 - c[0]
        finish(j, scores(j), False)
        return c[0] + 1, live()

    lax.while_loop(lambda c: (c[0] < qi) & c[1], body, (jnp.minimum(qi, 1), live()))
    o_ref[0] = acc_ref[...].T.astype(o_ref.dtype)


def _sb_attn(q, k, vt, batch, seq):
    nk = seq // TILE
    return pl.pallas_call(
        _sb_attn_kernel,
        grid=(batch, nk),
        in_specs=[pl.BlockSpec((1, TILE, GROUP), lambda b, i: (b, i, 0)),
                  pl.BlockSpec((1, seq, GROUP), lambda b, i: (b, 0, 0)),
                  pl.BlockSpec((1, nk, GROUP, TILE), lambda b, i: (b, 0, 0, 0))],
        out_specs=pl.BlockSpec((1, TILE, GROUP), lambda b, i: (b, i, 0)),
        out_shape=jax.ShapeDtypeStruct((batch, seq, GROUP), BF16),
        scratch_shapes=[pltpu.VMEM((N_HEADS * TILE, GROUP), BF16), pltpu.VMEM((N_HEADS, 1, TILE), F32),
                        pltpu.VMEM((GROUP, TILE), F32)],
        compiler_params=_params(2),
        name="sb_attn",
    )(q, k, vt)


def _moba_attn_kernel(nk, q_ref, qf_ref, k_ref, vt_ref, km_ref, bias_ref, o_ref,
                      qbd_ref, sel_ref, *scratch):
    qi = pl.program_id(1)
    _store_head_masked_q(q_ref[0], qbd_ref)
    km = km_ref[0, :, 0, :]
    lane = lax.broadcasted_iota(jnp.int32, (nk, GROUP), 1)
    km_hi, km_lo = _split_bf16(jnp.concatenate(
        [jnp.where((lane >= h * HEAD_DIM) & (lane < (h + 1) * HEAD_DIM), km, 0.0) for h in range(N_HEADS)], axis=0))
    qf_hi, qf_lo = _split_bf16(qf_ref[0])
    gate_all = _dot_nt(km_hi, qf_hi) + (_dot_nt(km_hi, qf_lo) + _dot_nt(km_lo, qf_hi))
    blk = lax.broadcasted_iota(jnp.int32, (nk, TILE), 0)
    for h in range(N_HEADS):
        gate = gate_all[h * nk:(h + 1) * nk, :]
        rival = jnp.where(blk < qi, gate, -jnp.inf)
        beaten = jnp.zeros((nk, TILE), jnp.int32)
        for jp in range(nk):
            gj = rival[jp:jp + 1, :]
            beaten = beaten + jnp.where((gj > gate) | ((gj == gate) & (jp < blk)), 1, 0)
        keep = (blk < qi) & (beaten < MOBA_TOPK)
        sel_ref[h] = jnp.where(keep, 0.0, -jnp.inf)

    def score_fn(h, j, s_ref):
        s_ref[:, _head_cols(h)] = _dot_nt(k_ref[0, _key_rows(j), :], qbd_ref[_head_cols(h), :])

    def adjust_fn(h, j, s, first):
        if first:
            return s + bias_ref[0, h], None
        d = jnp.minimum(qi - j, N_BIAS_TILES - 1)
        return s + bias_ref[d, h], sel_ref[h, pl.ds(j, 1), :]

    _softmax_sweep(qi, score_fn, adjust_fn, vt_ref, o_ref, scratch)


def _moba_attn(q, qf, k, vt, kmean, bias, batch, seq):
    nk = seq // TILE
    return pl.pallas_call(
        functools.partial(_moba_attn_kernel, nk),
        grid=(batch, nk),
        in_specs=[pl.BlockSpec((1, TILE, GROUP), lambda b, i: (b, i, 0)),
                  pl.BlockSpec((1, TILE, GROUP), lambda b, i: (b, i, 0)),
                  pl.BlockSpec((1, seq, GROUP), lambda b, i: (b, 0, 0)),
                  pl.BlockSpec((1, nk, GROUP, TILE), lambda b, i: (b, 0, 0, 0)),
                  pl.BlockSpec((1, nk, 1, GROUP), lambda b, i: (b, 0, 0, 0)),
                  pl.BlockSpec((N_BIAS_TILES, N_HEADS, TILE, TILE), lambda b, i: (0, 0, 0, 0))],
        out_specs=pl.BlockSpec((1, TILE, GROUP), lambda b, i: (b, i, 0)),
        out_shape=jax.ShapeDtypeStruct((batch, seq, GROUP), BF16),
        scratch_shapes=[pltpu.VMEM((N_HEADS * TILE, GROUP), BF16), pltpu.VMEM((N_HEADS, nk, TILE), F32)]
        + _sweep_scratch(),
        compiler_params=_params(2),
        name="moba_attn",
    )(q, qf, k, vt, kmean, bias)


def _conv_module(i, u_ref, halo_ref, dw_ref, dwb_ref, lng_ref, lnb_ref, pw_ref, pwb_ref, ext_ref, shift_ref):
    um = u_ref[0]
    uh = halo_ref[0]
    xh = uh[:, :GROUP] * jax.nn.sigmoid(uh[:, GROUP:])
    ext_ref[0:CONV_HALO, :] = jnp.where(i == 0, 0.0, xh)
    ext_ref[CONV_HALO:, :] = um[:, :GROUP] * jax.nn.sigmoid(um[:, GROUP:])
    base = CONV_HALO - (CONV_WIDTH - 1)
    for b in range(1, 8):
        shift_ref[b - 1] = ext_ref[b:b + CONV_SHIFT_ROWS, :]
    y = jnp.zeros((ROW_TILE, GROUP), F32)
    for w in range(CONV_WIDTH):
        a, b = divmod(base + w, 8)
        rows = slice(8 * a, 8 * a + ROW_TILE)
        tap = ext_ref[rows, :] if b == 0 else shift_ref[b - 1, rows, :]
        y = y + tap * dw_ref[w:w + 1, :]
    y = y + dwb_ref[...]
    mu = jnp.mean(y, axis=-1, keepdims=True)
    yc = y - mu
    var = jnp.mean(yc * yc, axis=-1, keepdims=True)
    yn = yc * lax.rsqrt(var + EPS) * lng_ref[...] + lnb_ref[...]
    sw = yn * jax.nn.sigmoid(yn)
    return _dot(sw.astype(BF16), pw_ref[...]) + pwb_ref[...]


def _out_proj_kernel(u_ref, halo_ref, dw_ref, dwb_ref, lng_ref, lnb_ref, pw_ref, pwb_ref,
                     oa_ref, ob_ref, oc_ref, gate_ref, w_ref, g_ref, x_ref, o_ref, ext_ref, shift_ref):
    od = _conv_module(pl.program_id(1), u_ref, halo_ref, dw_ref, dwb_ref, lng_ref, lnb_ref, pw_ref, pwb_ref,
                      ext_ref, shift_ref)
    sg = gate_ref[0].astype(F32)
    mix =jnp.concatenate([oa_ref[0].astype(F32), ob_ref[0].astype(F32), oc_ref[0].astype(F32), od], axis=-1)
    y = _dot((mix * sg).astype(BF16), w_ref[...])
    ms = jnp.mean(y * y, axis=-1, keepdims=True)
    o_ref[0] = x_ref[0] + y * lax.rsqrt(ms + EPS) * g_ref[...]


def _out_proj(u_conv, conv_params, oa, ob, oc, gate, w, g, x3):
    batch, seq, _ = x3.shape
    per_b = seq // ROW_TILE
    halo_per_tile = ROW_TILE // CONV_HALO
    row = lambda cols: pl.BlockSpec((1, ROW_TILE, cols), lambda b, i: (b, i, 0))
    full = lambda shp: pl.BlockSpec(shp, lambda b, i: (0,) * len(shp))
    return pl.pallas_call(
        _out_proj_kernel,
        grid=(batch, per_b),
        in_specs=[row(2 * GROUP),
                  pl.BlockSpec((1, CONV_HALO, 2 * GROUP),
                               lambda b, i: (b, jnp.maximum(i * halo_per_tile - 1, 0), 0)),
                  full((CONV_HALO, GROUP)), full((1, GROUP)), full((1, GROUP)), full((1, GROUP)),
                  full((GROUP, GROUP)), full((1, GROUP)),
                  row(GROUP), row(GROUP), row(GROUP), row(D_MIX),
                  full((D_MIX, D_MODEL)), full((1, D_MODEL)), row(D_MODEL)],
        out_specs=row(D_MODEL),
        out_shape=jax.ShapeDtypeStruct((batch, seq, D_MODEL), F32),
        scratch_shapes=[pltpu.VMEM((CONV_HALO + ROW_TILE, GROUP), F32),
                        pltpu.VMEM((7, CONV_SHIFT_ROWS, GROUP), F32)],
        compiler_params=_params(2),
        name="out_proj",
    )(u_conv, u_conv, *conv_params, oa, ob, oc, gate, w, g, x3)


def _prep_in_proj_weights(w_in):
    c = 0
    cq = w_in[..., c:c + Q_RANK]; c += Q_RANK
    ckv = w_in[..., c:c + KV_RANK]; c += KV_RANK
    kr = w_in[..., c:c + ROPE_DIM]; c += ROPE_DIM
    sbq, sbk, sbv = (w_in[..., c + i * GROUP:c + (i + 1) * GROUP] for i in range(3)); c += 3 * GROUP
    mbq, mbk, mbv = (w_in[..., c + i * GROUP:c + (i + 1) * GROUP] for i in range(3)); c += 3 * GROUP
    conv = w_in[..., c:c + 2 * GROUP]; c += 2 * GROUP
    gate = w_in[..., c:c + D_MIX]
    kr_swapped = jnp.concatenate([kr[..., ROPE_HALF:], kr[..., :ROPE_HALF]], axis=-1)
    scale = HEAD_DIM ** -0.5
    main = jnp.concatenate([cq, kr, kr_swapped, ckv, sbq * scale, sbk, sbv, mbq * scale, mbk, conv, gate], axis=-1)
    return main.astype(BF16), jnp.swapaxes(mbv, 1, 2).astype(BF16)


def _prep_mla_weights(w_uq, w_ukv):
    depth = w_uq.shape[0]
    qh = w_uq.reshape(depth, Q_RANK, N_HEADS, HEAD_DIM + ROPE_DIM)
    nope, r1, r2 = qh[..., :HEAD_DIM], qh[..., HEAD_DIM:HEAD_DIM + ROPE_HALF], qh[..., HEAD_DIM + ROPE_HALF:]
    zpad = jnp.zeros((depth, Q_RANK, N_HEADS, MLA_HEAD_PAD - HEAD_DIM - ROPE_DIM), w_uq.dtype)
    row_pad = ((0, 0), (0, 256 - Q_RANK), (0, 0))
    wq = jnp.pad(jnp.concatenate([nope, r1, r2, zpad], -1).reshape(depth, Q_RANK, -1), row_pad)
    wqs = jnp.pad(jnp.concatenate([jnp.zeros_like(nope), r2, r1, zpad], -1).reshape(depth, Q_RANK, -1), row_pad)
    kvh = w_ukv.reshape(depth, KV_RANK, N_HEADS, 2 * HEAD_DIM)
    k_nope, v = kvh[..., :HEAD_DIM], kvh[..., HEAD_DIM:]
    wkn = jnp.concatenate([k_nope, jnp.zeros_like(k_nope)], -1).reshape(depth, KV_RANK, -1)
    wvt = jnp.swapaxes(v.reshape(depth, KV_RANK, GROUP), 1, 2)
    return wq.astype(BF16), wqs.astype(BF16), wkn.astype(BF16), wvt.astype(BF16)


def _rope_tables(seq):
    freqs = ROPE_THETA ** (-jnp.arange(ROPE_HALF, dtype=F32) / ROPE_HALF)
    ang = jnp.arange(seq, dtype=jnp.int32).astype(F32)[:, None] * freqs[None, :]
    cos, sin = jnp.cos(ang), jnp.sin(ang)
    ones, zeros = jnp.ones((seq, HEAD_DIM), F32), jnp.zeros((seq, HEAD_DIM), F32)
    ztail = jnp.zeros((seq, MLA_HEAD_PAD - HEAD_DIM - ROPE_DIM), F32)
    c_tab = jnp.concatenate([ones, cos, cos, ztail], axis=-1)
    s_tab = jnp.concatenate([zeros, -sin, sin, ztail], axis=-1)
    ck_tab = jnp.concatenate([zeros, cos, cos, ztail], axis=-1)
    return c_tab, s_tab, ck_tab


def kernel(x, pre_norm_g, w_in, mla_q_norm_g, mla_w_uq, mla_kv_norm_g, mla_w_ukv, rel_bias, conv_dw_w, conv_dw_b, conv_ln_g, conv_ln_b, conv_pw_w, conv_pw_b, w_out, post_norm_g):
    batch, seq, d_model = x.shape
    depth = w_in.shape[0]
    assert d_model == D_MODEL and seq % ROW_TILE == 0 and ROW_TILE % TILE == 0
    n = batch * seq

    w_main, w_vt = _prep_in_proj_weights(w_in)
    wq, wqs, wkn, wvt_mla = _prep_mla_weights(mla_w_uq, mla_w_ukv)
    c_tab, s_tab, ck_tab = _rope_tables(seq)
    mla_scale = (HEAD_DIM + ROPE_DIM) ** -0.5 * LOG2E
    cq_tab, sq_tab = c_tab * mla_scale, s_tab * mla_scale
    gq = jnp.pad(mla_q_norm_g, ((0, 0), (0, 256 - Q_RANK)))
    dw = jnp.pad(conv_dw_w, ((0, 0), (0, CONV_HALO - CONV_WIDTH), (0, 0)))
    pw = conv_pw_w.astype(BF16)
    w_o = w_out.astype(BF16)
    bias = _bias_tiles(rel_bias)

    seq_major = lambda a: a.reshape(batch, seq, a.shape[-1])
    for l in range(depth):
        mla_params = (gq[l][None], mla_kv_norm_g[l][None], wq[l], wqs[l], wkn[l], wvt_mla[l],
                      cq_tab, sq_tab, ck_tab, s_tab)
        (qcat, kcat, mla_vt, sbq, sbk, mbq, mbqf, mbk, u_conv, u_gate, sbvt, mbvt, kmean) = _in_proj(
            x.reshape(n, D_MODEL), pre_norm_g[l][None], w_main[l], w_vt[l], mla_params, batch, seq)
        o_a = _mla_attn(qcat, kcat, mla_vt, batch, seq)
        o_b = _sb_attn(seq_major(sbq), seq_major(sbk), sbvt, batch, seq)
        o_c = _moba_attn(seq_major(mbq), seq_major(mbqf), seq_major(mbk), mbvt, kmean, bias, batch, seq)
        conv_params = (dw[l], conv_dw_b[l][None], conv_ln_g[l][None], conv_ln_b[l][None], pw[l],
                       conv_pw_b[l][None])
        x = _out_proj(seq_major(u_conv), conv_params, o_a, o_b, o_c, seq_major(u_gate), w_o[l],
                      post_norm_g[l][None], x)
    return x
```

```python
import functools
import math

import jax
import jax.numpy as jnp
from jax import lax
from jax.experimental import pallas as pl
from jax.experimental.pallas import tpu as pltpu

F32 = jnp.float32
BF16 = jnp.bfloat16

D_MODEL = 1024
HEAD_DIM = 64
N_HEADS = 4
GROUP = N_HEADS * HEAD_DIM
D_MIX = 4 * GROUP
Q_RANK = 192
KV_RANK = 128
ROPE_DIM = 32
ROPE_HALF = ROPE_DIM // 2
ROPE_THETA = 10000.0
MLA_HEAD_PAD = 128
CONV_WIDTH = 31
NUM_BUCKETS = 32
MAX_DISTANCE = 1024
MOBA_TOPK = 3
EPS = 1e-6

LANES = 128
TILE = 256
ROW_TILE = 1024
CONV_HALO = 32
CONV_SHIFT_ROWS = CONV_HALO + ROW_TILE - 8
N_BIAS_TILES = 6
MLA_U = 384
MAIN_COLS = MLA_U + 5 * GROUP + 2 * GROUP + D_MIX
VMEM_LIMIT = 56 * 1024 * 1024
SB_EXP2_UNDERFLOW = -150.0
SIGN_BIT = 0x80000000
SCAN_RUN = TILE // 8
ONES_ROWS = 16
ACC_ROWS = HEAD_DIM + 8
LOG2E = math.log2(math.e)

_NT = (((1,), (1,)), ((), ()))


def _params(n_axes):
    return pltpu.CompilerParams(dimension_semantics=("arbitrary",) * n_axes,
                                vmem_limit_bytes=VMEM_LIMIT)


def _dot(a, b):
    return jnp.dot(a, b, preferred_element_type=F32)


def _dot_nt(a, b):
    return lax.dot_general(a, b, _NT, preferred_element_type=F32)


def _bias_tiles_kernel(rb_ref, o_ref):
    d = pl.program_id(0)
    row = lax.broadcasted_iota(jnp.int32, (TILE, TILE), 0)
    col = lax.broadcasted_iota(jnp.int32, (TILE, TILE), 1)
    n = jnp.maximum(d * TILE + col - row, 0)
    max_exact = NUM_BUCKETS // 2
    n_large = jnp.maximum(n, max_exact).astype(F32)
    large = max_exact + (jnp.log(n_large / max_exact) / math.log(MAX_DISTANCE / max_exact)
                         * (NUM_BUCKETS - max_exact)).astype(jnp.int32)
    large = jnp.minimum(large, NUM_BUCKETS - 1)
    bucket = jnp.where(n < max_exact, n, large)
    for h in range(N_HEADS):
        acc = jnp.zeros((TILE, TILE), F32)
        for b in range(NUM_BUCKETS):
            acc = jnp.where(bucket == b, rb_ref[b, h], acc)
        o_ref[0, h] = acc * LOG2E


def _bias_tiles(rel_bias):
    return pl.pallas_call(
        _bias_tiles_kernel,
        grid=(N_BIAS_TILES,),
        in_specs=[pl.BlockSpec(memory_space=pltpu.SMEM)],
        out_specs=pl.BlockSpec((1, N_HEADS, TILE, TILE), lambda d: (d, 0, 0, 0)),
        out_shape=jax.ShapeDtypeStruct((N_BIAS_TILES, N_HEADS, TILE, TILE), F32),
        compiler_params=_params(1),
        name="t5_bias_tiles",
    )(rel_bias)


def _in_proj_kernel(x_ref, g_ref, w_ref, wvt_ref, *rest):
    mla_params, rest = rest[:9], rest[9:]
    (mlaq_ref, mlak_ref, mlavt_ref, sbq_ref, sbk_ref, mbq_ref, mbqf_ref, mbk_ref, conv_ref, gate_ref,
     sbvt_ref, mbvt_ref, kmean_ref, kv_ref, kvscan_ref) = rest
    x = x_ref[...]
    ms = jnp.mean(x * x, axis=-1, keepdims=True)
    h = (x * lax.rsqrt(ms + EPS) * g_ref[...]).astype(BF16)

    def mm(lo, hi):
        return _dot(h, w_ref[:, lo:hi])

    c = 0
    _mla_prep(mm(c, c + MLA_U), *mla_params, mlaq_ref, mlak_ref, mlavt_ref); c += MLA_U
    sbq_ref[...] = (mm(c, c + GROUP) * LOG2E).astype(BF16); c += GROUP
    kv = mm(c, c + 2 * GROUP); c += 2 * GROUP
    n_chunks = 2 * GROUP // LANES
    for cc in range(n_chunks):
        kv_ref[cc] = kv[:, cc * LANES:(cc + 1) * LANES]
    for cc in range(n_chunks):
        for t in range(ROW_TILE // TILE):
            for i in range(SCAN_RUN):
                kvscan_ref[t * TILE + i * 8:t * TILE + (i + 1) * 8, cc * LANES:(cc + 1) * LANES] = (
                    kv_ref[cc, pl.ds(t * TILE + i, 8, stride=SCAN_RUN), :])
    sbk_ref[...] = kvscan_ref[:, :GROUP].astype(BF16)
    for t in range(ROW_TILE // TILE):
        sbvt_ref[0, t] = kvscan_ref[t * TILE:(t + 1) * TILE, GROUP:].T.astype(BF16)
    qf = mm(c, c + GROUP); c += GROUP
    mbqf_ref[...] = qf
    mbq_ref[...] = (qf * LOG2E).astype(BF16)
    kf = mm(c, c + GROUP); c += GROUP
    mbk_ref[...] = kf.astype(BF16)
    conv_ref[...] = mm(c, c + 2 * GROUP); c += 2 * GROUP
    gate = mm(c, c + D_MIX)
    gate_ref[...] = (gate * jax.nn.sigmoid(gate)).astype(BF16)
    mb_vt = _dot_nt(wvt_ref[...], h)
    for t in range(ROW_TILE // TILE):
        rows = slice(t * TILE, (t + 1) * TILE)
        kmean_ref[0, t] = jnp.mean(kf[rows], axis=0, keepdims=True)
        mbvt_ref[0, t] = mb_vt[:, rows].astype(BF16)


def _in_proj(x2, g, w_main, w_vt, mla_params, batch, seq):
    n = batch * seq
    nk = seq // TILE
    per_b = seq // ROW_TILE
    tpr = ROW_TILE // TILE
    row = lambda cols: pl.BlockSpec((ROW_TILE, cols), lambda i: (i, 0))
    full = lambda shp: pl.BlockSpec(shp, lambda i: (0,) * len(shp))
    tab = pl.BlockSpec((ROW_TILE, MLA_HEAD_PAD), lambda i: (i % per_b, 0))
    head_spec = pl.BlockSpec((1, N_HEADS, ROW_TILE, MLA_HEAD_PAD), lambda i: (i // per_b, 0, i % per_b, 0))
    vt_spec = pl.BlockSpec((1, tpr, GROUP, TILE), lambda i: (i // per_b, i % per_b, 0, 0))
    km_spec = pl.BlockSpec((1, tpr, 1, GROUP), lambda i: (i // per_b, i % per_b, 0, 0))
    sd = jax.ShapeDtypeStruct
    heads = sd((batch, N_HEADS, seq, MLA_HEAD_PAD), BF16)
    vt_tiles = sd((batch, nk, GROUP, TILE), BF16)
    return pl.pallas_call(
        _in_proj_kernel,
        grid=(n // ROW_TILE,),
        in_specs=[row(D_MODEL), full((1, D_MODEL)), full((D_MODEL, MAIN_COLS)),
                  full((GROUP, D_MODEL)),
                  full((1, 256)), full((1, KV_RANK)),
                  full((256, N_HEADS * MLA_HEAD_PAD)),
                  full((KV_RANK, N_HEADS * MLA_HEAD_PAD)), full((GROUP, KV_RANK)),
                  tab, tab, tab, tab],
        out_specs=[head_spec, head_spec, vt_spec,
                   row(GROUP), row(GROUP), row(GROUP), row(GROUP), row(GROUP),
                   row(2 * GROUP), row(D_MIX), vt_spec, vt_spec, km_spec],
        out_shape=[heads, heads, vt_tiles,
                   sd((n, GROUP), BF16), sd((n, GROUP), BF16),
                   sd((n, GROUP), BF16), sd((n, GROUP), F32), sd((n, GROUP), BF16),
                   sd((n, 2 * GROUP), F32), sd((n, D_MIX), BF16),
                   vt_tiles, vt_tiles, sd((batch, nk, 1, GROUP), F32)],
        scratch_shapes=[pltpu.VMEM((2 * GROUP // LANES, ROW_TILE, LANES), F32),
                        pltpu.VMEM((ROW_TILE, 2 * GROUP), F32)],
        compiler_params=_params(1),
        name="in_proj",
    )(x2, g, w_main, w_vt, *mla_params)


def _mla_prep(u, gq_ref, gkv_ref, wq_ref, wkn_ref, wvt_ref,
              cq_ref, sq_ref, ck_ref, sk_ref, q_ref, k_ref, vt_ref):
    cq = u[:, 0:256]
    lane = lax.broadcasted_iota(jnp.int32, cq.shape, 1)
    msq = jnp.sum(jnp.where(lane < Q_RANK, cq * cq, 0.0), axis=-1, keepdims=True) * (1.0 / Q_RANK)
    cqn = (cq * lax.rsqrt(msq + EPS) * gq_ref[...]).astype(BF16)
    qa = _dot(cqn, wq_ref[...])
    ckv = u[:, 256:384]
    msk = jnp.mean(ckv * ckv, axis=-1, keepdims=True)
    ckvn = (ckv * lax.rsqrt(msk + EPS) * gkv_ref[...]).astype(BF16)
    kn = _dot(ckvn, wkn_ref[...])
    tail = u[:, LANES:2 * LANES]
    kr = tail * ck_ref[...] + pltpu.roll(tail, LANES - ROPE_DIM, 1) * sk_ref[...]
    cq_t = cq_ref[...]
    sq_t = sq_ref[...]
    first_half = lax.broadcasted_iota(jnp.int32, cq_t.shape, 1) < HEAD_DIM + ROPE_HALF
    for h in range(N_HEADS):
        sl = slice(h * MLA_HEAD_PAD, (h + 1) * MLA_HEAD_PAD)
        qh = qa[:, sl]
        swapped = jnp.where(first_half, pltpu.roll(qh, LANES - ROPE_HALF, 1), pltpu.roll(qh, ROPE_HALF, 1))
        q_ref[0, h] = (qh * cq_t + swapped * sq_t).astype(BF16)
        k_ref[0, h] = (kn[:, sl] + kr).astype(BF16)
    vt = _dot_nt(wvt_ref[...], ckvn)
    for t in range(ROW_TILE // TILE):
        vt_ref[0, t] = vt[:, t * TILE:(t + 1) * TILE].astype(BF16)


def _head_rows(h):
    return slice(h * HEAD_DIM, (h + 1) * HEAD_DIM)


def _head_cols(h):
    return slice(h * TILE, (h + 1) * TILE)


def _key_rows(j):
    return pl.ds(pl.multiple_of(j * TILE, TILE), TILE)


def _store_head_masked_q(q, qbd_ref):
    lane = lax.broadcasted_iota(jnp.int32, (TILE, GROUP), 1)
    for h in range(N_HEADS):
        in_head = (lane >= h * HEAD_DIM) & (lane < (h + 1) * HEAD_DIM)
        qbd_ref[h * TILE:(h + 1) * TILE, :] = jnp.where(in_head, q, jnp.zeros_like(q))


def _sweep_scratch():
    s_buf = pltpu.VMEM((TILE, N_HEADS * TILE), F32)
    p_buf = pltpu.VMEM((N_HEADS, TILE, TILE), BF16)
    stat = pltpu.VMEM((N_HEADS, 1, TILE), F32)
    return [s_buf, s_buf, p_buf, p_buf, stat, stat, stat, pltpu.VMEM((N_HEADS, ACC_ROWS, TILE), F32)]


def _softmax_sweep(qi, n_tiles, prepare_fn, score_fn, adjust_fn, vt_ref, o_ref, scratch):
    s_a, s_b, p_a, p_b, al_a, al_b, m_ref, acc_ref = scratch
    row = lax.broadcasted_iota(jnp.int32, (TILE, TILE), 0)
    col = lax.broadcasted_iota(jnp.int32, (TILE, TILE), 1)
    ones_rows = jnp.ones((ONES_ROWS, TILE), BF16)

    def key_tile(t):
        if isinstance(t, int) and t == 0:
            return qi
        return jnp.minimum(t - 1, jnp.maximum(qi - 1, 0))

    def softmax(h, t, s_ref, p_ref, al_ref):
        first = isinstance(t, int) and t == 0
        s, pen = adjust_fn(h, key_tile(t), s_ref[:, _head_cols(h)], first)
        if first:
            s = jnp.where(row <= col, s, -jnp.inf)
        s = s.astype(BF16)
        m_tile = jnp.max(s, axis=0, keepdims=True).astype(F32)
        if first:
            m_new = m_tile
            p = jnp.exp2(s - m_new.astype(BF16))
        else:
            dead = jnp.where(t <= qi, 0.0, -jnp.inf)
            pen = dead if pen is None else pen + dead
            m_old = m_ref[h]
            m_new = jnp.maximum(m_old, m_tile + pen)
            al_ref[h] = jnp.exp2(m_old - m_new)
            p = jnp.exp2(s - (m_new - pen).astype(BF16))
        m_ref[h] = m_new
        p_ref[h] = p

    def weighted_values(h, t, p_ref, al_ref):
        first = isinstance(t, int) and t == 0
        vt = jnp.concatenate([vt_ref[0, key_tile(t), _head_rows(h), :], ones_rows], axis=0)
        pv = _dot(vt, p_ref[h])[:ACC_ROWS]
        acc_ref[h] = pv if first else acc_ref[h] * al_ref[h] + pv

    def half(t, cur, nxt):
        (s_cur, p_cur, al_cur), (s_nxt, p_nxt, al_nxt) = cur, nxt
        for h in range(N_HEADS):
            score_fn(h, key_tile(t + 1), s_nxt, qi)
            if not (isinstance(t, int) and t == 0):
                weighted_values(h, t - 1, p_nxt, al_nxt)
            softmax(h, t, s_cur, p_cur, al_cur)

    buf_a, buf_b = (s_a, p_a, al_a), (s_b, p_b, al_b)

    def diagonal_scores(qt):
        prepare_fn(qt)
        for h in range(N_HEADS):
            score_fn(h, qt, s_a, qt)

    @pl.when(qi == 0)
    def _():
        diagonal_scores(qi)

    half(0, buf_a, buf_b)
    half(1, buf_b, buf_a)

    def pair(k, carry):
        half(2 * k, buf_a, buf_b)
        half(2 * k + 1, buf_b, buf_a)
        return carry

    def pairs_from(first, count):
        def body(k, carry):
            for i in range(count):
                pair(first + count * k + i, carry)
            return carry
        return body

    n_pairs = qi // 2 + 1
    n_quad = (n_pairs - 1) // 4
    n_double = ((n_pairs - 1) % 4) // 2
    lax.fori_loop(0, n_quad, pairs_from(1, 4), 0)
    lax.fori_loop(0, n_double, pairs_from(4 * n_quad + 1, 2), 0)
    lax.fori_loop(4 * n_quad + 2 * n_double + 1, n_pairs, pair, 0)
    def finish(next_tile_follows):
        for h in range(N_HEADS):
            weighted_values(h, 2 * n_pairs - 1, p_b, al_b)
        if next_tile_follows:
            diagonal_scores(qi + 1)
        out_t = [acc_ref[h, :HEAD_DIM, :] / acc_ref[h, HEAD_DIM:HEAD_DIM + 1, :] for h in range(N_HEADS)]
        o_ref[0] = jnp.concatenate(out_t, axis=0).T.astype(o_ref.dtype)

    @pl.when(qi == n_tiles - 1)
    def _():
        finish(False)

    @pl.when(qi < n_tiles - 1)
    def _():
        finish(True)


def _mla_attn_kernel(n_tiles, q_ref, k_ref, vt_ref, o_ref, *scratch):
    qi = pl.program_id(1)

    def score_fn(h, j, s_ref, qt):
        s_ref[:, _head_cols(h)] = _dot_nt(k_ref[0, h, _key_rows(j), :], q_ref[0, h, _key_rows(qt), :])

    _softmax_sweep(qi, n_tiles, lambda qt: None, score_fn, lambda h, j, s, first: (s, None),
                   vt_ref, o_ref, scratch)


def _mla_attn(q, k, vt, batch, seq):
    nk = seq // TILE
    return pl.pallas_call(
        functools.partial(_mla_attn_kernel, nk),
        grid=(batch, nk),
        in_specs=[pl.BlockSpec((1, N_HEADS, seq, MLA_HEAD_PAD), lambda b, i: (b, 0, 0, 0)),
                  pl.BlockSpec((1, N_HEADS, seq, MLA_HEAD_PAD), lambda b, i: (b, 0, 0, 0)),
                  pl.BlockSpec((1, nk, GROUP, TILE), lambda b, i: (b, 0, 0, 0))],
        out_specs=pl.BlockSpec((1, TILE, GROUP), lambda b, i: (b, i, 0)),
        out_shape=jax.ShapeDtypeStruct((batch, seq, GROUP), BF16),
        scratch_shapes=_sweep_scratch(),
        compiler_params=_params(2),
        name="mla_attn",
    )(q, k, vt)


def _split_bf16(x):
    hi = x.astype(BF16)
    lo = (x - hi.astype(F32)).astype(BF16)
    return hi, lo


def _suffix_scan(x, carry):
    slabs = [x[i * 8:(i + 1) * 8, :] for i in range(SCAN_RUN)]
    rest_of_run = [None] * SCAN_RUN
    rest_of_run[SCAN_RUN - 1] = slabs[SCAN_RUN - 1]
    for i in range(SCAN_RUN - 2, -1, -1):
        rest_of_run[i] = rest_of_run[i + 1] + slabs[i]
    run_total = rest_of_run[0]
    sub = lax.broadcasted_iota(jnp.int32, run_total.shape, 0)
    from_here = run_total
    for d in (1, 2, 4):
        from_here = from_here + jnp.where(sub + d < 8, pltpu.roll(from_here, 8 - d, 0), 0.0)
    later_runs = from_here - run_total + carry
    suffix = jnp.concatenate([rest_of_run[i] + later_runs for i in range(SCAN_RUN)], axis=0)
    return suffix, from_here[0:1, :]


def _sb_attn_kernel(q_ref, k_ref, vt_ref, o_ref, qbd_ref, carry_ref, acc_ref):
    qi = pl.program_id(1)
    _store_head_masked_q(q_ref[0], qbd_ref)
    row = lax.broadcasted_iota(jnp.int32, (TILE, TILE), 0)
    col = lax.broadcasted_iota(jnp.int32, (TILE, TILE), 1)
    token = (row & 7) * SCAN_RUN + (row >> 3)
    past = token < col

    def scores(j):
        return _dot_nt(k_ref[0, _key_rows(j), :], qbd_ref[...])

    def finish(j, z_all, diag):
        for h in range(N_HEADS):
            hs = _head_rows(h)
            z = z_all[:, h * TILE:(h + 1) * TILE]
            neg_abs = lax.bitcast_convert_type(
                lax.bitcast_convert_type(z, jnp.uint32) | jnp.uint32(SIGN_BIT), F32)
            w = jnp.maximum(z, 0.0) + jnp.log2(1.0 + jnp.exp2(neg_abs))
            if diag:
                w = jnp.where(past, w, 0.0)
                w_from_here, tile_sum = _suffix_scan(w, jnp.zeros((1, TILE), F32))
                a = jnp.where(past, jnp.exp2(z - w_from_here), 0.0)
                carry_ref[h] = tile_sum
                acc_ref[hs, :] = _dot(vt_ref[0, j, hs, :], a.astype(BF16))
            else:
                carry = carry_ref[h]
                w_from_here, tile_sum = _suffix_scan(w, carry)
                a = jnp.exp2(z - w_from_here)
                carry_ref[h] = carry + tile_sum
                acc_ref[hs, :] = acc_ref[hs, :] + _dot(vt_ref[0, j, hs, :], a.astype(BF16))

    def live():
        return jnp.min(carry_ref[...]) < -SB_EXP2_UNDERFLOW

    @pl.when(qi == 0)
    def _():
        finish(qi, scores(qi), True)

    @pl.when(qi > 0)
    def _():
        z_diag, z_prev = scores(qi), scores(qi - 1)
        finish(qi, z_diag, True)
        finish(qi - 1, z_prev, False)

    def body(c):
        j = qi - 1 - c[0]
        finish(j, scores(j), False)
        return c[0] + 1, live()

    lax.while_loop(lambda c: (c[0] < qi) & c[1], body, (jnp.minimum(qi, 1), live()))
    o_ref[0] = acc_ref[...].T.astype(o_ref.dtype)


def _sb_attn(q, k, vt, batch, seq):
    nk = seq // TILE
    return pl.pallas_call(
        _sb_attn_kernel,
        grid=(batch, nk),
        in_specs=[pl.BlockSpec((1, TILE, GROUP), lambda b, i: (b, i, 0)),
                  pl.BlockSpec((1, seq, GROUP), lambda b, i: (b, 0, 0)),
                  pl.BlockSpec((1, nk, GROUP, TILE), lambda b, i: (b, 0, 0, 0))],
        out_specs=pl.BlockSpec((1, TILE, GROUP), lambda b, i: (b, i, 0)),
        out_shape=jax.ShapeDtypeStruct((batch, seq, GROUP), BF16),
        scratch_shapes=[pltpu.VMEM((N_HEADS * TILE, GROUP), BF16), pltpu.VMEM((N_HEADS, 1, TILE), F32),
                        pltpu.VMEM((GROUP, TILE), F32)],
        compiler_params=_params(2),
        name="sb_attn",
    )(q, k, vt)


def _moba_attn_kernel(nk, q_ref, qf_ref, k_ref, vt_ref, km_ref, bias_ref, o_ref,
                      qbd_ref, sel_ref, *scratch):
    qi = pl.program_id(1)
    km = km_ref[0, :, 0, :]
    lane = lax.broadcasted_iota(jnp.int32, (nk, GROUP), 1)
    km_hi, km_lo = _split_bf16(jnp.concatenate(
        [jnp.where((lane >= h * HEAD_DIM) & (lane < (h + 1) * HEAD_DIM), km, 0.0) for h in range(N_HEADS)], axis=0))
    qf_hi, qf_lo = _split_bf16(qf_ref[0])
    gate_all = _dot_nt(km_hi, qf_hi) + (_dot_nt(km_hi, qf_lo) + _dot_nt(km_lo, qf_hi))
    blk = lax.broadcasted_iota(jnp.int32, (nk, TILE), 0)
    for h in range(N_HEADS):
        gate = gate_all[h * nk:(h + 1) * nk, :]
        rival = jnp.where(blk < qi, gate, -jnp.inf)
        beaten = jnp.zeros((nk, TILE), jnp.int32)
        for jp in range(nk):
            gj = rival[jp:jp + 1, :]
            beaten = beaten + jnp.where((gj > gate) | ((gj == gate) & (jp < blk)), 1, 0)
        keep = (blk < qi) & (beaten < MOBA_TOPK)
        sel_ref[h] = jnp.where(keep, 0.0, -jnp.inf)

    def prepare_fn(qt):
        _store_head_masked_q(q_ref[0, _key_rows(qt), :], qbd_ref)

    def score_fn(h, j, s_ref, qt):
        s_ref[:, _head_cols(h)] = _dot_nt(k_ref[0, _key_rows(j), :], qbd_ref[_head_cols(h), :])

    def adjust_fn(h, j, s, first):
        if first:
            return s + bias_ref[0, h], None
        d = jnp.minimum(qi - j, N_BIAS_TILES - 1)
        return s + bias_ref[d, h], sel_ref[h, pl.ds(j, 1), :]

    _softmax_sweep(qi, nk, prepare_fn, score_fn, adjust_fn, vt_ref, o_ref, scratch)


def _moba_attn(q, qf, k, vt, kmean, bias, batch, seq):
    nk = seq // TILE
    return pl.pallas_call(
        functools.partial(_moba_attn_kernel, nk),
        grid=(batch, nk),
        in_specs=[pl.BlockSpec((1, seq, GROUP), lambda b, i: (b, 0, 0)),
                  pl.BlockSpec((1, TILE, GROUP), lambda b, i: (b, i, 0)),
                  pl.BlockSpec((1, seq, GROUP), lambda b, i: (b, 0, 0)),
                  pl.BlockSpec((1, nk, GROUP, TILE), lambda b, i: (b, 0, 0, 0)),
                  pl.BlockSpec((1, nk, 1, GROUP), lambda b, i: (b, 0, 0, 0)),
                  pl.BlockSpec((N_BIAS_TILES, N_HEADS, TILE, TILE), lambda b, i: (0, 0, 0, 0))],
        out_specs=pl.BlockSpec((1, TILE, GROUP), lambda b, i: (b, i, 0)),
        out_shape=jax.ShapeDtypeStruct((batch, seq, GROUP), BF16),
        scratch_shapes=[pltpu.VMEM((N_HEADS * TILE, GROUP), BF16), pltpu.VMEM((N_HEADS, nk, TILE), F32)]
        + _sweep_scratch(),
        compiler_params=_params(2),
        name="moba_attn",
    )(q, qf, k, vt, kmean, bias)


def _conv_module(i, u_ref, halo_ref, dw_ref, dwb_ref, lng_ref, lnb_ref, pw_ref, pwb_ref, ext_ref, shift_ref):
    um = u_ref[0]
    uh = halo_ref[0]
    xh = uh[:, :GROUP] * jax.nn.sigmoid(uh[:, GROUP:])
    ext_ref[0:CONV_HALO, :] = jnp.where(i == 0, 0.0, xh)
    ext_ref[CONV_HALO:, :] = um[:, :GROUP] * jax.nn.sigmoid(um[:, GROUP:])
    base = CONV_HALO - (CONV_WIDTH - 1)
    for b in range(1, 8):
        shift_ref[b - 1] = ext_ref[b:b + CONV_SHIFT_ROWS, :]
    y = jnp.zeros((ROW_TILE, GROUP), F32)
    for w in range(CONV_WIDTH):
        a, b = divmod(base + w, 8)
        rows = slice(8 * a, 8 * a + ROW_TILE)
        tap = ext_ref[rows, :] if b == 0 else shift_ref[b - 1, rows, :]
        y = y + tap * dw_ref[w:w + 1, :]
    y = y + dwb_ref[...]
    mu = jnp.mean(y, axis=-1, keepdims=True)
    yc = y - mu
    var = jnp.mean(yc * yc, axis=-1, keepdims=True)
    yn = yc * lax.rsqrt(var + EPS) * lng_ref[...] + lnb_ref[...]
    sw = yn * jax.nn.sigmoid(yn)
    return _dot(sw.astype(BF16), pw_ref[...]) + pwb_ref[...]


def _out_proj_kernel(u_ref, halo_ref, dw_ref, dwb_ref, lng_ref, lnb_ref, pw_ref, pwb_ref,
                     oa_ref, ob_ref, oc_ref, gate_ref, w_ref, g_ref, x_ref, o_ref, ext_ref, shift_ref):
    od = _conv_module(pl.program_id(1), u_ref, halo_ref, dw_ref, dwb_ref, lng_ref, lnb_ref, pw_ref, pwb_ref,
                      ext_ref, shift_ref)
    sg = gate_ref[0].astype(F32)
    mix =jnp.concatenate([oa_ref[0].astype(F32), ob_ref[0].astype(F32), oc_ref[0].astype(F32), od], axis=-1)
    y = _dot((mix * sg).astype(BF16), w_ref[...])
    ms = jnp.mean(y * y, axis=-1, keepdims=True)
    o_ref[0] = x_ref[0] + y * lax.rsqrt(ms + EPS) * g_ref[...]


def _out_proj(u_conv, conv_params, oa, ob, oc, gate, w, g, x3):
    batch, seq, _ = x3.shape
    per_b = seq // ROW_TILE
    halo_per_tile = ROW_TILE // CONV_HALO
    row = lambda cols: pl.BlockSpec((1, ROW_TILE, cols), lambda b, i: (b, i, 0))
    full = lambda shp: pl.BlockSpec(shp, lambda b, i: (0,) * len(shp))
    return pl.pallas_call(
        _out_proj_kernel,
        grid=(batch, per_b),
        in_specs=[row(2 * GROUP),
                  pl.BlockSpec((1, CONV_HALO, 2 * GROUP),
                               lambda b, i: (b, jnp.maximum(i * halo_per_tile - 1, 0), 0)),
                  full((CONV_HALO, GROUP)), full((1, GROUP)), full((1, GROUP)), full((1, GROUP)),
                  full((GROUP, GROUP)), full((1, GROUP)),
                  row(GROUP), row(GROUP), row(GROUP), row(D_MIX),
                  full((D_MIX, D_MODEL)), full((1, D_MODEL)), row(D_MODEL)],
        out_specs=row(D_MODEL),
        out_shape=jax.ShapeDtypeStruct((batch, seq, D_MODEL), F32),
        scratch_shapes=[pltpu.VMEM((CONV_HALO + ROW_TILE, GROUP), F32),
                        pltpu.VMEM((7, CONV_SHIFT_ROWS, GROUP), F32)],
        compiler_params=_params(2),
        name="out_proj",
    )(u_conv, u_conv, *conv_params, oa, ob, oc, gate, w, g, x3)


def _prep_in_proj_weights(w_in):
    c = 0
    cq = w_in[..., c:c + Q_RANK]; c += Q_RANK
    ckv = w_in[..., c:c + KV_RANK]; c += KV_RANK
    kr = w_in[..., c:c + ROPE_DIM]; c += ROPE_DIM
    sbq, sbk, sbv = (w_in[..., c + i * GROUP:c + (i + 1) * GROUP] for i in range(3)); c += 3 * GROUP
    mbq, mbk, mbv = (w_in[..., c + i * GROUP:c + (i + 1) * GROUP] for i in range(3)); c += 3 * GROUP
    conv = w_in[..., c:c + 2 * GROUP]; c += 2 * GROUP
    gate = w_in[..., c:c + D_MIX]
    kr_swapped = jnp.concatenate([kr[..., ROPE_HALF:], kr[..., :ROPE_HALF]], axis=-1)
    scale = HEAD_DIM ** -0.5
    main = jnp.concatenate([cq, kr, kr_swapped, ckv, sbq * scale, sbk, sbv, mbq * scale, mbk, conv, gate], axis=-1)
    return main.astype(BF16), jnp.swapaxes(mbv, 1, 2).astype(BF16)


def _prep_mla_weights(w_uq, w_ukv):
    depth = w_uq.shape[0]
    qh = w_uq.reshape(depth, Q_RANK, N_HEADS, HEAD_DIM + ROPE_DIM)
    nope, r1, r2 = qh[..., :HEAD_DIM], qh[..., HEAD_DIM:HEAD_DIM + ROPE_HALF], qh[..., HEAD_DIM + ROPE_HALF:]
    zpad = jnp.zeros((depth, Q_RANK, N_HEADS, MLA_HEAD_PAD - HEAD_DIM - ROPE_DIM), w_uq.dtype)
    row_pad = ((0, 0), (0, 256 - Q_RANK), (0, 0))
    wq = jnp.pad(jnp.concatenate([nope, r1, r2, zpad], -1).reshape(depth, Q_RANK, -1), row_pad)
    kvh = w_ukv.reshape(depth, KV_RANK, N_HEADS, 2 * HEAD_DIM)
    k_nope, v = kvh[..., :HEAD_DIM], kvh[..., HEAD_DIM:]
    wkn = jnp.concatenate([k_nope, jnp.zeros_like(k_nope)], -1).reshape(depth, KV_RANK, -1)
    wvt = jnp.swapaxes(v.reshape(depth, KV_RANK, GROUP), 1, 2)
    return wq.astype(BF16), wkn.astype(BF16), wvt.astype(BF16)


def _rope_tables(seq):
    freqs = ROPE_THETA ** (-jnp.arange(ROPE_HALF, dtype=F32) / ROPE_HALF)
    ang = jnp.arange(seq, dtype=jnp.int32).astype(F32)[:, None] * freqs[None, :]
    cos, sin = jnp.cos(ang), jnp.sin(ang)
    ones, zeros = jnp.ones((seq, HEAD_DIM), F32), jnp.zeros((seq, HEAD_DIM), F32)
    ztail = jnp.zeros((seq, MLA_HEAD_PAD - HEAD_DIM - ROPE_DIM), F32)
    c_tab = jnp.concatenate([ones, cos, cos, ztail], axis=-1)
    s_tab = jnp.concatenate([zeros, -sin, sin, ztail], axis=-1)
    ck_tab = jnp.concatenate([zeros, cos, cos, ztail], axis=-1)
    return c_tab, s_tab, ck_tab


def kernel(x, pre_norm_g, w_in, mla_q_norm_g, mla_w_uq, mla_kv_norm_g, mla_w_ukv, rel_bias, conv_dw_w, conv_dw_b, conv_ln_g, conv_ln_b, conv_pw_w, conv_pw_b, w_out, post_norm_g):
    batch, seq, d_model = x.shape
    depth = w_in.shape[0]
    assert d_model == D_MODEL and seq % ROW_TILE == 0 and ROW_TILE % TILE == 0
    n = batch * seq

    w_main, w_vt = _prep_in_proj_weights(w_in)
    wq, wkn, wvt_mla = _prep_mla_weights(mla_w_uq, mla_w_ukv)
    c_tab, s_tab, ck_tab = _rope_tables(seq)
    mla_scale = (HEAD_DIM + ROPE_DIM) ** -0.5 * LOG2E
    cq_tab, sq_tab = c_tab * mla_scale, s_tab * mla_scale
    gq = jnp.pad(mla_q_norm_g, ((0, 0), (0, 256 - Q_RANK)))
    dw = jnp.pad(conv_dw_w, ((0, 0), (0, CONV_HALO - CONV_WIDTH), (0, 0)))
    pw = conv_pw_w.astype(BF16)
    w_o = w_out.astype(BF16)
    bias = _bias_tiles(rel_bias)

    seq_major = lambda a: a.reshape(batch, seq, a.shape[-1])
    for l in range(depth):
        mla_params = (gq[l][None], mla_kv_norm_g[l][None], wq[l], wkn[l], wvt_mla[l],
                      cq_tab, sq_tab, ck_tab, s_tab)
        (qcat, kcat, mla_vt, sbq, sbk, mbq, mbqf, mbk, u_conv, u_gate, sbvt, mbvt, kmean) = _in_proj(
            x.reshape(n, D_MODEL), pre_norm_g[l][None], w_main[l], w_vt[l], mla_params, batch, seq)
        o_a = _mla_attn(qcat, kcat, mla_vt, batch, seq)
        o_b = _sb_attn(seq_major(sbq), seq_major(sbk), sbvt, batch, seq)
        o_c = _moba_attn(seq_major(mbq), seq_major(mbqf), seq_major(mbk), mbvt, kmean, bias, batch, seq)
        conv_params = (dw[l], conv_dw_b[l][None], conv_ln_g[l][None], conv_ln_b[l][None], pw[l],
                       conv_pw_b[l][None])
        x = _out_proj(seq_major(u_conv), conv_params, o_a, o_b, o_c, seq_major(u_gate), w_o[l],
                      post_norm_g[l][None], x)
    return x
```

```python
import functools
import math

import jax
import jax.numpy as jnp
from jax import lax
from jax.experimental import pallas as pl
from jax.experimental.pallas import tpu as pltpu

F32 = jnp.float32
BF16 = jnp.bfloat16

D_MODEL = 1024
HEAD_DIM = 64
N_HEADS = 4
GROUP = N_HEADS * HEAD_DIM
D_MIX = 4 * GROUP
Q_RANK = 192
KV_RANK = 128
ROPE_DIM = 32
ROPE_HALF = ROPE_DIM // 2
ROPE_THETA = 10000.0
MLA_HEAD_PAD = 128
CONV_WIDTH = 31
NUM_BUCKETS = 32
MAX_DISTANCE = 1024
MOBA_TOPK = 3
EPS = 1e-6

LANES = 128
TILE = 256
ROW_TILE = 1024
CONV_HALO = 32
CONV_SHIFT_ROWS = CONV_HALO + ROW_TILE - 8
N_BIAS_TILES = 6
MLA_U = 384
MAIN_COLS = MLA_U + 5 * GROUP + 2 * GROUP + D_MIX
VMEM_LIMIT = 56 * 1024 * 1024
SB_EXP2_UNDERFLOW = -150.0
SIGN_BIT = 0x80000000
SCAN_RUN = TILE // 8
ONES_ROWS = 16
ACC_ROWS = HEAD_DIM + 8
LOG2E = math.log2(math.e)

_NT = (((1,), (1,)), ((), ()))


def _params(n_axes):
    return pltpu.CompilerParams(dimension_semantics=("arbitrary",) * n_axes,
                                vmem_limit_bytes=VMEM_LIMIT)


def _dot(a, b):
    return jnp.dot(a, b, preferred_element_type=F32)


def _dot_nt(a, b):
    return lax.dot_general(a, b, _NT, preferred_element_type=F32)


def _bias_tiles_kernel(rb_ref, o_ref):
    d = pl.program_id(0)
    row = lax.broadcasted_iota(jnp.int32, (TILE, TILE), 0)
    col = lax.broadcasted_iota(jnp.int32, (TILE, TILE), 1)
    n = jnp.maximum(d * TILE + col - row, 0)
    max_exact = NUM_BUCKETS // 2
    n_large = jnp.maximum(n, max_exact).astype(F32)
    large = max_exact + (jnp.log(n_large / max_exact) / math.log(MAX_DISTANCE / max_exact)
                         * (NUM_BUCKETS - max_exact)).astype(jnp.int32)
    large = jnp.minimum(large, NUM_BUCKETS - 1)
    bucket = jnp.where(n < max_exact, n, large)
    for h in range(N_HEADS):
        acc = jnp.zeros((TILE, TILE), F32)
        for b in range(NUM_BUCKETS):
            acc = jnp.where(bucket == b, rb_ref[b, h], acc)
        o_ref[0, h] = acc * LOG2E


def _bias_tiles(rel_bias):
    return pl.pallas_call(
        _bias_tiles_kernel,
        grid=(N_BIAS_TILES,),
        in_specs=[pl.BlockSpec(memory_space=pltpu.SMEM)],
        out_specs=pl.BlockSpec((1, N_HEADS, TILE, TILE), lambda d: (d, 0, 0, 0)),
        out_shape=jax.ShapeDtypeStruct((N_BIAS_TILES, N_HEADS, TILE, TILE), F32),
        compiler_params=_params(1),
        name="t5_bias_tiles",
    )(rel_bias)


def _in_proj_kernel(x_ref, g_ref, w_ref, wvt_ref, *rest):
    mla_params, rest = rest[:9], rest[9:]
    (mlaq_ref, mlak_ref, mlavt_ref, sbq_ref, sbk_ref, mbq_ref, mbqf_ref, mbk_ref, conv_ref, gate_ref,
     sbvt_ref, mbvt_ref, kmean_ref, kv_ref, kvscan_ref) = rest
    x = x_ref[...]
    ms = jnp.mean(x * x, axis=-1, keepdims=True)
    h = (x * lax.rsqrt(ms + EPS) * g_ref[...]).astype(BF16)

    def mm(lo, hi):
        return _dot(h, w_ref[:, lo:hi])

    c = 0
    _mla_prep(mm(c, c + MLA_U), *mla_params, mlaq_ref, mlak_ref, mlavt_ref); c += MLA_U
    sbq_ref[...] = (mm(c, c + GROUP) * LOG2E).astype(BF16); c += GROUP
    kv = mm(c, c + 2 * GROUP); c += 2 * GROUP
    n_chunks = 2 * GROUP // LANES
    for cc in range(n_chunks):
        kv_ref[cc] = kv[:, cc * LANES:(cc + 1) * LANES]
    for cc in range(n_chunks):
        for t in range(ROW_TILE // TILE):
            for i in range(SCAN_RUN):
                kvscan_ref[t * TILE + i * 8:t * TILE + (i + 1) * 8, cc * LANES:(cc + 1) * LANES] = (
                    kv_ref[cc, pl.ds(t * TILE + i, 8, stride=SCAN_RUN), :])
    sbk_ref[...] = kvscan_ref[:, :GROUP].astype(BF16)
    for t in range(ROW_TILE // TILE):
        sbvt_ref[0, t] = kvscan_ref[t * TILE:(t + 1) * TILE, GROUP:].T.astype(BF16)
    qf = mm(c, c + GROUP); c += GROUP
    mbqf_ref[...] = qf
    mbq_ref[...] = (qf * LOG2E).astype(BF16)
    kf = mm(c, c + GROUP); c += GROUP
    mbk_ref[...] = kf.astype(BF16)
    conv_ref[...] = mm(c, c + 2 * GROUP); c += 2 * GROUP
    gate = mm(c, c + D_MIX)
    gate_ref[...] = (gate * jax.nn.sigmoid(gate)).astype(BF16)
    mb_vt = _dot_nt(wvt_ref[...], h)
    for t in range(ROW_TILE // TILE):
        rows = slice(t * TILE, (t + 1) * TILE)
        kmean_ref[0, t] = jnp.mean(kf[rows], axis=0, keepdims=True)
        mbvt_ref[0, t] = mb_vt[:, rows].astype(BF16)


def _in_proj(x2, g, w_main, w_vt, mla_params, batch, seq):
    n = batch * seq
    nk = seq // TILE
    per_b = seq // ROW_TILE
    tpr = ROW_TILE // TILE
    row = lambda cols: pl.BlockSpec((ROW_TILE, cols), lambda i: (i, 0))
    full = lambda shp: pl.BlockSpec(shp, lambda i: (0,) * len(shp))
    tab = pl.BlockSpec((ROW_TILE, MLA_HEAD_PAD), lambda i: (i % per_b, 0))
    head_spec = pl.BlockSpec((1, N_HEADS, ROW_TILE, MLA_HEAD_PAD), lambda i: (i // per_b, 0, i % per_b, 0))
    vt_spec = pl.BlockSpec((1, tpr, GROUP, TILE), lambda i: (i // per_b, i % per_b, 0, 0))
    km_spec = pl.BlockSpec((1, tpr, 1, GROUP), lambda i: (i // per_b, i % per_b, 0, 0))
    sd = jax.ShapeDtypeStruct
    heads = sd((batch, N_HEADS, seq, MLA_HEAD_PAD), BF16)
    vt_tiles = sd((batch, nk, GROUP, TILE), BF16)
    return pl.pallas_call(
        _in_proj_kernel,
        grid=(n // ROW_TILE,),
        in_specs=[row(D_MODEL), full((1, D_MODEL)), full((D_MODEL, MAIN_COLS)),
                  full((GROUP, D_MODEL)),
                  full((1, 256)), full((1, KV_RANK)),
                  full((256, N_HEADS * MLA_HEAD_PAD)),
                  full((KV_RANK, N_HEADS * MLA_HEAD_PAD)), full((GROUP, KV_RANK)),
                  tab, tab, tab, tab],
        out_specs=[head_spec, head_spec, vt_spec,
                   row(GROUP), row(GROUP), row(GROUP), row(GROUP), row(GROUP),
                   row(2 * GROUP), row(D_MIX), vt_spec, vt_spec, km_spec],
        out_shape=[heads, heads, vt_tiles,
                   sd((n, GROUP), BF16), sd((n, GROUP), BF16),
                   sd((n, GROUP), BF16), sd((n, GROUP), F32), sd((n, GROUP), BF16),
                   sd((n, 2 * GROUP), F32), sd((n, D_MIX), BF16),
                   vt_tiles, vt_tiles, sd((batch, nk, 1, GROUP), F32)],
        scratch_shapes=[pltpu.VMEM((2 * GROUP // LANES, ROW_TILE, LANES), F32),
                        pltpu.VMEM((ROW_TILE, 2 * GROUP), F32)],
        compiler_params=_params(1),
        name="in_proj",
    )(x2, g, w_main, w_vt, *mla_params)


def _mla_prep(u, gq_ref, gkv_ref, wq_ref, wkn_ref, wvt_ref,
              cq_ref, sq_ref, ck_ref, sk_ref, q_ref, k_ref, vt_ref):
    cq = u[:, 0:256]
    lane = lax.broadcasted_iota(jnp.int32, cq.shape, 1)
    msq = jnp.sum(jnp.where(lane < Q_RANK, cq * cq, 0.0), axis=-1, keepdims=True) * (1.0 / Q_RANK)
    cqn = (cq * lax.rsqrt(msq + EPS) * gq_ref[...]).astype(BF16)
    qa = _dot(cqn, wq_ref[...])
    ckv = u[:, 256:384]
    msk = jnp.mean(ckv * ckv, axis=-1, keepdims=True)
    ckvn = (ckv * lax.rsqrt(msk + EPS) * gkv_ref[...]).astype(BF16)
    kn = _dot(ckvn, wkn_ref[...])
    tail = u[:, LANES:2 * LANES]
    kr = tail * ck_ref[...] + pltpu.roll(tail, LANES - ROPE_DIM, 1) * sk_ref[...]
    cq_t = cq_ref[...]
    sq_t = sq_ref[...]
    first_half = lax.broadcasted_iota(jnp.int32, cq_t.shape, 1) < HEAD_DIM + ROPE_HALF
    for h in range(N_HEADS):
        sl = slice(h * MLA_HEAD_PAD, (h + 1) * MLA_HEAD_PAD)
        qh = qa[:, sl]
        swapped = jnp.where(first_half, pltpu.roll(qh, LANES - ROPE_HALF, 1), pltpu.roll(qh, ROPE_HALF, 1))
        q_ref[0, h] = (qh * cq_t + swapped * sq_t).astype(BF16)
        k_ref[0, h] = (kn[:, sl] + kr).astype(BF16)
    vt = _dot_nt(wvt_ref[...], ckvn)
    for t in range(ROW_TILE // TILE):
        vt_ref[0, t] = vt[:, t * TILE:(t + 1) * TILE].astype(BF16)


def _head_rows(h):
    return slice(h * HEAD_DIM, (h + 1) * HEAD_DIM)


def _head_cols(h):
    return slice(h * TILE, (h + 1) * TILE)


def _key_rows(j):
    return pl.ds(pl.multiple_of(j * TILE, TILE), TILE)


def _store_head_masked_q(q, qbd_ref):
    lane = lax.broadcasted_iota(jnp.int32, (TILE, GROUP), 1)
    for h in range(N_HEADS):
        in_head = (lane >= h * HEAD_DIM) & (lane < (h + 1) * HEAD_DIM)
        qbd_ref[h * TILE:(h + 1) * TILE, :] = jnp.where(in_head, q, jnp.zeros_like(q))


def _sweep_scratch():
    s_buf = pltpu.VMEM((TILE, N_HEADS * TILE), F32)
    p_buf = pltpu.VMEM((N_HEADS, TILE, TILE), BF16)
    stat = pltpu.VMEM((N_HEADS, 1, TILE), F32)
    return [s_buf, s_buf, p_buf, p_buf, stat, stat, stat, pltpu.VMEM((N_HEADS, ACC_ROWS, TILE), F32)]


def _softmax_sweep(qi, score_fn, adjust_fn, vt_ref, o_ref, scratch):
    s_a, s_b, p_a, p_b, al_a, al_b, m_ref, acc_ref = scratch
    row = lax.broadcasted_iota(jnp.int32, (TILE, TILE), 0)
    col = lax.broadcasted_iota(jnp.int32, (TILE, TILE), 1)
    ones_rows = jnp.ones((ONES_ROWS, TILE), BF16)

    def key_tile(t):
        if isinstance(t, int) and t == 0:
            return qi
        return jnp.minimum(t - 1, jnp.maximum(qi - 1, 0))

    def softmax(h, t, s_ref, p_ref, al_ref):
        first = isinstance(t, int) and t == 0
        s, pen = adjust_fn(h, key_tile(t), s_ref[:, _head_cols(h)], first)
        if first:
            s = jnp.where(row <= col, s, -jnp.inf)
        s = s.astype(BF16)
        m_tile = jnp.max(s, axis=0, keepdims=True).astype(F32)
        if first:
            m_new = m_tile
            p = jnp.exp2(s - m_new.astype(BF16))
        else:
            dead = jnp.where(t <= qi, 0.0, -jnp.inf)
            pen = dead if pen is None else pen + dead
            m_old = m_ref[h]
            m_new = jnp.maximum(m_old, m_tile + pen)
            al_ref[h] = jnp.exp2(m_old - m_new)
            p = jnp.exp2(s - (m_new - pen).astype(BF16))
        m_ref[h] = m_new
        p_ref[h] = p

    def weighted_values(h, t, p_ref, al_ref):
        first = isinstance(t, int) and t == 0
        vt = jnp.concatenate([vt_ref[0, key_tile(t), _head_rows(h), :], ones_rows], axis=0)
        pv = _dot(vt, p_ref[h])[:ACC_ROWS]
        acc_ref[h] = pv if first else acc_ref[h] * al_ref[h] + pv

    def half(t, cur, nxt):
        (s_cur, p_cur, al_cur), (s_nxt, p_nxt, al_nxt) = cur, nxt
        for h in range(N_HEADS):
            score_fn(h, key_tile(t + 1), s_nxt)
            if not (isinstance(t, int) and t == 0):
                weighted_values(h, t - 1, p_nxt, al_nxt)
            softmax(h, t, s_cur, p_cur, al_cur)

    buf_a, buf_b = (s_a, p_a, al_a), (s_b, p_b, al_b)
    for h in range(N_HEADS):
        score_fn(h, key_tile(0), s_a)
    half(0, buf_a, buf_b)
    half(1, buf_b, buf_a)

    def pair(k, carry):
        half(2 * k, buf_a, buf_b)
        half(2 * k + 1, buf_b, buf_a)
        return carry

    def pairs_from(first, count):
        def body(k, carry):
            for i in range(count):
                pair(first + count * k + i, carry)
            return carry
        return body

    n_pairs = qi // 2 + 1
    n_quad = (n_pairs - 1) // 4
    n_double = ((n_pairs - 1) % 4) // 2
    lax.fori_loop(0, n_quad, pairs_from(1, 4), 0)
    lax.fori_loop(0, n_double, pairs_from(4 * n_quad + 1, 2), 0)
    lax.fori_loop(4 * n_quad + 2 * n_double + 1, n_pairs, pair, 0)
    for h in range(N_HEADS):
        weighted_values(h, 2 * n_pairs - 1, p_b, al_b)
    out_t = [acc_ref[h, :HEAD_DIM, :] / acc_ref[h, HEAD_DIM:HEAD_DIM + 1, :] for h in range(N_HEADS)]
    o_ref[0] = jnp.concatenate(out_t, axis=0).T.astype(o_ref.dtype)


def _mla_attn_kernel(q_ref, k_ref, vt_ref, o_ref, *scratch):
    qi = pl.program_id(1)

    def score_fn(h, j, s_ref):
        s_ref[:, _head_cols(h)] = _dot_nt(k_ref[0, h, _key_rows(j), :], q_ref[0, h])

    _softmax_sweep(qi, score_fn, lambda h, j, s, first: (s, None), vt_ref, o_ref, scratch)


def _mla_attn(q, k, vt, batch, seq):
    nk = seq // TILE
    return pl.pallas_call(
        _mla_attn_kernel,
        grid=(batch, nk),
        in_specs=[pl.BlockSpec((1, N_HEADS, TILE, MLA_HEAD_PAD), lambda b, i: (b, 0, i, 0)),
                  pl.BlockSpec((1, N_HEADS, seq, MLA_HEAD_PAD), lambda b, i: (b, 0, 0, 0)),
                  pl.BlockSpec((1, nk, GROUP, TILE), lambda b, i: (b, 0, 0, 0))],
        out_specs=pl.BlockSpec((1, TILE, GROUP), lambda b, i: (b, i, 0)),
        out_shape=jax.ShapeDtypeStruct((batch, seq, GROUP), BF16),
        scratch_shapes=_sweep_scratch(),
        compiler_params=_params(2),
        name="mla_attn",
    )(q, k, vt)


def _split_bf16(x):
    hi = x.astype(BF16)
    lo = (x - hi.astype(F32)).astype(BF16)
    return hi, lo


def _suffix_scan(x, carry):
    slabs = [x[i * 8:(i + 1) * 8, :] for i in range(SCAN_RUN)]
    rest_of_run = [None] * SCAN_RUN
    rest_of_run[SCAN_RUN - 1] = slabs[SCAN_RUN - 1]
    for i in range(SCAN_RUN - 2, -1, -1):
        rest_of_run[i] = rest_of_run[i + 1] + slabs[i]
    run_total = rest_of_run[0]
    sub = lax.broadcasted_iota(jnp.int32, run_total.shape, 0)
    from_here = run_total
    for d in (1, 2, 4):
        from_here = from_here + jnp.where(sub + d < 8, pltpu.roll(from_here, 8 - d, 0), 0.0)
    later_runs = from_here - run_total + carry
    suffix = jnp.concatenate([rest_of_run[i] + later_runs for i in range(SCAN_RUN)], axis=0)
    return suffix, from_here[0:1, :]


def _sb_attn_kernel(q_ref, k_ref, vt_ref, o_ref, qbd_ref, carry_ref, acc_ref):
    qi = pl.program_id(1)
    _store_head_masked_q(q_ref[0], qbd_ref)
    row = lax.broadcasted_iota(jnp.int32, (TILE, TILE), 0)
    col = lax.broadcasted_iota(jnp.int32, (TILE, TILE), 1)
    token = (row & 7) * SCAN_RUN + (row >> 3)
    past = token < col

    def scores(j):
        return _dot_nt(k_ref[0, _key_rows(j), :], qbd_ref[...])

    def finish(j, z_all, diag):
        for h in range(N_HEADS):
            hs = _head_rows(h)
            z = z_all[:, h * TILE:(h + 1) * TILE]
            neg_abs = lax.bitcast_convert_type(
                lax.bitcast_convert_type(z, jnp.uint32) | jnp.uint32(SIGN_BIT), F32)
            w = jnp.maximum(z, 0.0) + jnp.log2(1.0 + jnp.exp2(neg_abs))
            if diag:
                w = jnp.where(past, w, 0.0)
                w_from_here, tile_sum = _suffix_scan(w, jnp.zeros((1, TILE), F32))
                a = jnp.where(past, jnp.exp2(z - w_from_here), 0.0)
                carry_ref[h] = tile_sum
                acc_ref[hs, :] = _dot(vt_ref[0, j, hs, :], a.astype(BF16))
            else:
                carry = carry_ref[h]
                w_from_here, tile_sum = _suffix_scan(w, carry)
                a = jnp.exp2(z - w_from_here)
                carry_ref[h] = carry + tile_sum
                acc_ref[hs, :] = acc_ref[hs, :] + _dot(vt_ref[0, j, hs, :], a.astype(BF16))

    def live():
        return jnp.min(carry_ref[...]) < -SB_EXP2_UNDERFLOW

    @pl.when(qi == 0)
    def _():
        finish(qi, scores(qi), True)

    @pl.when(qi > 0)
    def _():
        z_diag, z_prev = scores(qi), scores(qi - 1)
        finish(qi, z_diag, True)
        finish(qi - 1, z_prev, False)

    def body(c):
        j = qi - 1 - c[0]
        finish(j, scores(j), False)
        return c[0] + 1, live()

    lax.while_loop(lambda c: (c[0] < qi) & c[1], body, (jnp.minimum(qi, 1), live()))
    o_ref[0] = acc_ref[...].T.astype(o_ref.dtype)


def _sb_attn(q, k, vt, batch, seq):
    nk = seq // TILE
    return pl.pallas_call(
        _sb_attn_kernel,
        grid=(batch, nk),
        in_specs=[pl.BlockSpec((1, TILE, GROUP), lambda b, i: (b, i, 0)),
                  pl.BlockSpec((1, seq, GROUP), lambda b, i: (b, 0, 0)),
                  pl.BlockSpec((1, nk, GROUP, TILE), lambda b, i: (b, 0, 0, 0))],
        out_specs=pl.BlockSpec((1, TILE, GROUP), lambda b, i: (b, i, 0)),
        out_shape=jax.ShapeDtypeStruct((batch, seq, GROUP), BF16),
        scratch_shapes=[pltpu.VMEM((N_HEADS * TILE, GROUP), BF16), pltpu.VMEM((N_HEADS, 1, TILE), F32),
                        pltpu.VMEM((GROUP, TILE), F32)],
        compiler_params=_params(2),
        name="sb_attn",
    )(q, k, vt)


def _moba_attn_kernel(nk, q_ref, qf_ref, k_ref, vt_ref, km_ref, bias_ref, o_ref,
                      qbd_ref, sel_ref, *scratch):
    qi = pl.program_id(1)
    _store_head_masked_q(q_ref[0], qbd_ref)
    km = km_ref[0, :, 0, :]
    lane = lax.broadcasted_iota(jnp.int32, (nk, GROUP), 1)
    km_hi, km_lo = _split_bf16(jnp.concatenate(
        [jnp.where((lane >= h * HEAD_DIM) & (lane < (h + 1) * HEAD_DIM), km, 0.0) for h in range(N_HEADS)], axis=0))
    qf_hi, qf_lo = _split_bf16(qf_ref[0])
    gate_all = _dot_nt(km_hi, qf_hi) + (_dot_nt(km_hi, qf_lo) + _dot_nt(km_lo, qf_hi))
    blk = lax.broadcasted_iota(jnp.int32, (nk, TILE), 0)
    for h in range(N_HEADS):
        gate = gate_all[h * nk:(h + 1) * nk, :]
        rival = jnp.where(blk < qi, gate, -jnp.inf)
        beaten = jnp.zeros((nk, TILE), jnp.int32)
        for jp in range(nk):
            gj = rival[jp:jp + 1, :]
            beaten = beaten + jnp.where((gj > gate) | ((gj == gate) & (jp < blk)), 1, 0)
        keep = (blk < qi) & (beaten < MOBA_TOPK)
        sel_ref[h] = jnp.where(keep, 0.0, -jnp.inf)

    def score_fn(h, j, s_ref):
        s_ref[:, _head_cols(h)] = _dot_nt(k_ref[0, _key_rows(j), :], qbd_ref[_head_cols(h), :])

    def adjust_fn(h, j, s, first):
        if first:
            return s + bias_ref[0, h], None
        d = jnp.minimum(qi - j, N_BIAS_TILES - 1)
        return s + bias_ref[d, h], sel_ref[h, pl.ds(j, 1), :]

    _softmax_sweep(qi, score_fn, adjust_fn, vt_ref, o_ref, scratch)


def _moba_attn(q, qf, k, vt, kmean, bias, batch, seq):
    nk = seq // TILE
    return pl.pallas_call(
        functools.partial(_moba_attn_kernel, nk),
        grid=(batch, nk),
        in_specs=[pl.BlockSpec((1, TILE, GROUP), lambda b, i: (b, i, 0)),
                  pl.BlockSpec((1, TILE, GROUP), lambda b, i: (b, i, 0)),
                  pl.BlockSpec((1, seq, GROUP), lambda b, i: (b, 0, 0)),
                  pl.BlockSpec((1, nk, GROUP, TILE), lambda b, i: (b, 0, 0, 0)),
                  pl.BlockSpec((1, nk, 1, GROUP), lambda b, i: (b, 0, 0, 0)),
                  pl.BlockSpec((N_BIAS_TILES, N_HEADS, TILE, TILE), lambda b, i: (0, 0, 0, 0))],
        out_specs=pl.BlockSpec((1, TILE, GROUP), lambda b, i: (b, i, 0)),
        out_shape=jax.ShapeDtypeStruct((batch, seq, GROUP), BF16),
        scratch_shapes=[pltpu.VMEM((N_HEADS * TILE, GROUP), BF16), pltpu.VMEM((N_HEADS, nk, TILE), F32)]
        + _sweep_scratch(),
        compiler_params=_params(2),
        name="moba_attn",
    )(q, qf, k, vt, kmean, bias)


def _conv_module(i, u_ref, halo_ref, dw_ref, dwb_ref, lng_ref, lnb_ref, pw_ref, pwb_ref, ext_ref, shift_ref):
    um = u_ref[0]
    uh = halo_ref[0]
    xh = uh[:, :GROUP] * jax.nn.sigmoid(uh[:, GROUP:])
    ext_ref[0:CONV_HALO, :] = jnp.where(i == 0, 0.0, xh)
    ext_ref[CONV_HALO:, :] = um[:, :GROUP] * jax.nn.sigmoid(um[:, GROUP:])
    base = CONV_HALO - (CONV_WIDTH - 1)
    for b in range(1, 8):
        shift_ref[b - 1] = ext_ref[b:b + CONV_SHIFT_ROWS, :]
    y = jnp.zeros((ROW_TILE, GROUP), F32)
    for w in range(CONV_WIDTH):
        a, b = divmod(base + w, 8)
        rows = slice(8 * a, 8 * a + ROW_TILE)
        tap = ext_ref[rows, :] if b == 0 else shift_ref[b - 1, rows, :]
        y = y + tap * dw_ref[w:w + 1, :]
    y = y + dwb_ref[...]
    mu = jnp.mean(y, axis=-1, keepdims=True)
    yc = y - mu
    var = jnp.mean(yc * yc, axis=-1, keepdims=True)
    yn = yc * lax.rsqrt(var + EPS) * lng_ref[...] + lnb_ref[...]
    sw = yn * jax.nn.sigmoid(yn)
    return _dot(sw.astype(BF16), pw_ref[...]) + pwb_ref[...]


def _out_proj_kernel(u_ref, halo_ref, dw_ref, dwb_ref, lng_ref, lnb_ref, pw_ref, pwb_ref,
                     oa_ref, ob_ref, oc_ref, gate_ref, w_ref, g_ref, x_ref, o_ref, ext_ref, shift_ref):
    od = _conv_module(pl.program_id(1), u_ref, halo_ref, dw_ref, dwb_ref, lng_ref, lnb_ref, pw_ref, pwb_ref,
                      ext_ref, shift_ref)
    sg = gate_ref[0].astype(F32)
    mix =jnp.concatenate([oa_ref[0].astype(F32), ob_ref[0].astype(F32), oc_ref[0].astype(F32), od], axis=-1)
    y = _dot((mix * sg).astype(BF16), w_ref[...])
    ms = jnp.mean(y * y, axis=-1, keepdims=True)
    o_ref[0] = x_ref[0] + y * lax.rsqrt(ms + EPS) * g_ref[...]


def _out_proj(u_conv, conv_params, oa, ob, oc, gate, w, g, x3):
    batch, seq, _ = x3.shape
    per_b = seq // ROW_TILE
    halo_per_tile = ROW_TILE // CONV_HALO
    row = lambda cols: pl.BlockSpec((1, ROW_TILE, cols), lambda b, i: (b, i, 0))
    full = lambda shp: pl.BlockSpec(shp, lambda b, i: (0,) * len(shp))
    return pl.pallas_call(
        _out_proj_kernel,
        grid=(batch, per_b),
        in_specs=[row(2 * GROUP),
                  pl.BlockSpec((1, CONV_HALO, 2 * GROUP),
                               lambda b, i: (b, jnp.maximum(i * halo_per_tile - 1, 0), 0)),
                  full((CONV_HALO, GROUP)), full((1, GROUP)), full((1, GROUP)), full((1, GROUP)),
                  full((GROUP, GROUP)), full((1, GROUP)),
                  row(GROUP), row(GROUP), row(GROUP), row(D_MIX),
                  full((D_MIX, D_MODEL)), full((1, D_MODEL)), row(D_MODEL)],
        out_specs=row(D_MODEL),
        out_shape=jax.ShapeDtypeStruct((batch, seq, D_MODEL), F32),
        scratch_shapes=[pltpu.VMEM((CONV_HALO + ROW_TILE, GROUP), F32),
                        pltpu.VMEM((7, CONV_SHIFT_ROWS, GROUP), F32)],
        compiler_params=_params(2),
        name="out_proj",
    )(u_conv, u_conv, *conv_params, oa, ob, oc, gate, w, g, x3)


def _prep_in_proj_weights(w_in):
    c = 0
    cq = w_in[..., c:c + Q_RANK]; c += Q_RANK
    ckv = w_in[..., c:c + KV_RANK]; c += KV_RANK
    kr = w_in[..., c:c + ROPE_DIM]; c += ROPE_DIM
    sbq, sbk, sbv = (w_in[..., c + i * GROUP:c + (i + 1) * GROUP] for i in range(3)); c += 3 * GROUP
    mbq, mbk, mbv = (w_in[..., c + i * GROUP:c + (i + 1) * GROUP] for i in range(3)); c += 3 * GROUP
    conv = w_in[..., c:c + 2 * GROUP]; c += 2 * GROUP
    gate = w_in[..., c:c + D_MIX]
    kr_swapped = jnp.concatenate([kr[..., ROPE_HALF:], kr[..., :ROPE_HALF]], axis=-1)
    scale = HEAD_DIM ** -0.5
    main = jnp.concatenate([cq, kr, kr_swapped, ckv, sbq * scale, sbk, sbv, mbq * scale, mbk, conv, gate], axis=-1)
    return main.astype(BF16), jnp.swapaxes(mbv, 1, 2).astype(BF16)


def _prep_mla_weights(w_uq, w_ukv):
    depth = w_uq.shape[0]
    qh = w_uq.reshape(depth, Q_RANK, N_HEADS, HEAD_DIM + ROPE_DIM)
    nope, r1, r2 = qh[..., :HEAD_DIM], qh[..., HEAD_DIM:HEAD_DIM + ROPE_HALF], qh[..., HEAD_DIM + ROPE_HALF:]
    zpad = jnp.zeros((depth, Q_RANK, N_HEADS, MLA_HEAD_PAD - HEAD_DIM - ROPE_DIM), w_uq.dtype)
    row_pad = ((0, 0), (0, 256 - Q_RANK), (0, 0))
    wq = jnp.pad(jnp.concatenate([nope, r1, r2, zpad], -1).reshape(depth, Q_RANK, -1), row_pad)
    kvh = w_ukv.reshape(depth, KV_RANK, N_HEADS, 2 * HEAD_DIM)
    k_nope, v = kvh[..., :HEAD_DIM], kvh[..., HEAD_DIM:]
    wkn = jnp.concatenate([k_nope, jnp.zeros_like(k_nope)], -1).reshape(depth, KV_RANK, -1)
    wvt = jnp.swapaxes(v.reshape(depth, KV_RANK, GROUP), 1, 2)
    return wq.astype(BF16), wkn.astype(BF16), wvt.astype(BF16)


def _rope_tables(seq):
    freqs = ROPE_THETA ** (-jnp.arange(ROPE_HALF, dtype=F32) / ROPE_HALF)
    ang = jnp.arange(seq, dtype=jnp.int32).astype(F32)[:, None] * freqs[None, :]
    cos, sin = jnp.cos(ang), jnp.sin(ang)
    ones, zeros = jnp.ones((seq, HEAD_DIM), F32), jnp.zeros((seq, HEAD_DIM), F32)
    ztail = jnp.zeros((seq, MLA_HEAD_PAD - HEAD_DIM - ROPE_DIM), F32)
    c_tab = jnp.concatenate([ones, cos, cos, ztail], axis=-1)
    s_tab = jnp.concatenate([zeros, -sin, sin, ztail], axis=-1)
    ck_tab = jnp.concatenate([zeros, cos, cos, ztail], axis=-1)
    return c_tab, s_tab, ck_tab


def kernel(x, pre_norm_g, w_in, mla_q_norm_g, mla_w_uq, mla_kv_norm_g, mla_w_ukv, rel_bias, conv_dw_w, conv_dw_b, conv_ln_g, conv_ln_b, conv_pw_w, conv_pw_b, w_out, post_norm_g):
    batch, seq, d_model = x.shape
    depth = w_in.shape[0]
    assert d_model == D_MODEL and seq % ROW_TILE == 0 and ROW_TILE % TILE == 0
    n = batch * seq

    w_main, w_vt = _prep_in_proj_weights(w_in)
    wq, wkn, wvt_mla = _prep_mla_weights(mla_w_uq, mla_w_ukv)
    c_tab, s_tab, ck_tab = _rope_tables(seq)
    mla_scale = (HEAD_DIM + ROPE_DIM) ** -0.5 * LOG2E
    cq_tab, sq_tab = c_tab * mla_scale, s_tab * mla_scale
    gq = jnp.pad(mla_q_norm_g, ((0, 0), (0, 256 - Q_RANK)))
    dw = jnp.pad(conv_dw_w, ((0, 0), (0, CONV_HALO - CONV_WIDTH), (0, 0)))
    pw = conv_pw_w.astype(BF16)
    w_o = w_out.astype(BF16)
    bias = _bias_tiles(rel_bias)

    seq_major = lambda a: a.reshape(batch, seq, a.shape[-1])
    for l in range(depth):
        mla_params = (gq[l][None], mla_kv_norm_g[l][None], wq[l], wkn[l], wvt_mla[l],
                      cq_tab, sq_tab, ck_tab, s_tab)
        (qcat, kcat, mla_vt, sbq, sbk, mbq, mbqf, mbk, u_conv, u_gate, sbvt, mbvt, kmean) = _in_proj(
            x.reshape(n, D_MODEL), pre_norm_g[l][None], w_main[l], w_vt[l], mla_params, batch, seq)
        o_a = _mla_attn(qcat, kcat, mla_vt, batch, seq)
        o_b = _sb_attn(seq_major(sbq), seq_major(sbk), sbvt, batch, seq)
        o_c = _moba_attn(seq_major(mbq), seq_major(mbqf), seq_major(mbk), mbvt, kmean, bias, batch, seq)
        conv_params = (dw[l], conv_dw_b[l][None], conv_ln_g[l][None], conv_ln_b[l][None], pw[l],
                       conv_pw_b[l][None])
        x = _out_proj(seq_major(u_conv), conv_params, o_a, o_b, o_c, seq_major(u_gate), w_o[l],
                      post_norm_g[l][None], x)
    return x
```

```python
import functools
import math

import jax
import jax.numpy as jnp
from jax import lax
from jax.experimental import pallas as pl
from jax.experimental.pallas import tpu as pltpu

F32 = jnp.float32
BF16 = jnp.bfloat16

D_MODEL = 1024
HEAD_DIM = 64
N_HEADS = 4
GROUP = N_HEADS * HEAD_DIM
D_MIX = 4 * GROUP
Q_RANK = 192
KV_RANK = 128
ROPE_DIM = 32
ROPE_HALF = ROPE_DIM // 2
ROPE_THETA = 10000.0
MLA_HEAD_PAD = 128
CONV_WIDTH = 31
NUM_BUCKETS = 32
MAX_DISTANCE = 1024
MOBA_TOPK = 3
EPS = 1e-6

LANES = 128
TILE = 256
ROW_TILE = 1024
CONV_HALO = 32
CONV_SHIFT_ROWS = CONV_HALO + ROW_TILE - 8
N_BIAS_TILES = 6
MLA_U = 384
MAIN_COLS = MLA_U + 5 * GROUP + 2 * GROUP + D_MIX
VMEM_LIMIT = 56 * 1024 * 1024
SB_EXP2_UNDERFLOW = -150.0
SIGN_BIT = 0x80000000
SCAN_RUN = TILE // 8
ONES_ROWS = 16
ACC_ROWS = HEAD_DIM + 8
LOG2E = math.log2(math.e)

_NT = (((1,), (1,)), ((), ()))


def _params(n_axes):
    return pltpu.CompilerParams(dimension_semantics=("arbitrary",) * n_axes,
                                vmem_limit_bytes=VMEM_LIMIT)


def _dot(a, b):
    return jnp.dot(a, b, preferred_element_type=F32)


def _dot_nt(a, b):
    return lax.dot_general(a, b, _NT, preferred_element_type=F32)


def _bias_tiles_kernel(rb_ref, o_ref):
    d = pl.program_id(0)
    row = lax.broadcasted_iota(jnp.int32, (TILE, TILE), 0)
    col = lax.broadcasted_iota(jnp.int32, (TILE, TILE), 1)
    n = jnp.maximum(d * TILE + col - row, 0)
    max_exact = NUM_BUCKETS // 2
    n_large = jnp.maximum(n, max_exact).astype(F32)
    large = max_exact + (jnp.log(n_large / max_exact) / math.log(MAX_DISTANCE / max_exact)
                         * (NUM_BUCKETS - max_exact)).astype(jnp.int32)
    large = jnp.minimum(large, NUM_BUCKETS - 1)
    bucket = jnp.where(n < max_exact, n, large)
    for h in range(N_HEADS):
        acc = jnp.zeros((TILE, TILE), F32)
        for b in range(NUM_BUCKETS):
            acc = jnp.where(bucket == b, rb_ref[b, h], acc)
        o_ref[0, h] = acc * LOG2E


def _bias_tiles(rel_bias):
    return pl.pallas_call(
        _bias_tiles_kernel,
        grid=(N_BIAS_TILES,),
        in_specs=[pl.BlockSpec(memory_space=pltpu.SMEM)],
        out_specs=pl.BlockSpec((1, N_HEADS, TILE, TILE), lambda d: (d, 0, 0, 0)),
        out_shape=jax.ShapeDtypeStruct((N_BIAS_TILES, N_HEADS, TILE, TILE), F32),
        compiler_params=_params(1),
        name="t5_bias_tiles",
    )(rel_bias)


def _in_proj_kernel(x_ref, g_ref, w_ref, wvt_ref, *rest):
    mla_params, rest = rest[:9], rest[9:]
    (mlaq_ref, mlak_ref, mlavt_ref, sbq_ref, sbk_ref, mbq_ref, mbqf_ref, mbk_ref, conv_ref, gate_ref,
     sbvt_ref, mbvt_ref, kmean_ref, kv_ref, kvscan_ref) = rest
    x = x_ref[...]
    ms = jnp.mean(x * x, axis=-1, keepdims=True)
    h = (x * lax.rsqrt(ms + EPS) * g_ref[...]).astype(BF16)

    def mm(lo, hi):
        return _dot(h, w_ref[:, lo:hi])

    c = 0
    _mla_prep(mm(c, c + MLA_U), *mla_params, mlaq_ref, mlak_ref, mlavt_ref); c += MLA_U
    sbq_ref[...] = (mm(c, c + GROUP) * LOG2E).astype(BF16); c += GROUP
    kv = mm(c, c + 2 * GROUP); c += 2 * GROUP
    n_chunks = 2 * GROUP // LANES
    for cc in range(n_chunks):
        kv_ref[cc] = kv[:, cc * LANES:(cc + 1) * LANES]
    for cc in range(n_chunks):
        for t in range(ROW_TILE // TILE):
            for i in range(SCAN_RUN):
                kvscan_ref[t * TILE + i * 8:t * TILE + (i + 1) * 8, cc * LANES:(cc + 1) * LANES] = (
                    kv_ref[cc, pl.ds(t * TILE + i, 8, stride=SCAN_RUN), :])
    sbk_ref[...] = kvscan_ref[:, :GROUP].astype(BF16)
    for t in range(ROW_TILE // TILE):
        sbvt_ref[0, t] = kvscan_ref[t * TILE:(t + 1) * TILE, GROUP:].T.astype(BF16)
    qf = mm(c, c + GROUP); c += GROUP
    mbqf_ref[...] = qf
    mbq_ref[...] = (qf * LOG2E).astype(BF16)
    kf = mm(c, c + GROUP); c += GROUP
    mbk_ref[...] = kf.astype(BF16)
    conv_ref[...] = mm(c, c + 2 * GROUP); c += 2 * GROUP
    gate = mm(c, c + D_MIX)
    gate_ref[...] = (gate * jax.nn.sigmoid(gate)).astype(BF16)
    mb_vt = _dot_nt(wvt_ref[...], h)
    for t in range(ROW_TILE // TILE):
        rows = slice(t * TILE, (t + 1) * TILE)
        kmean_ref[0, t] = jnp.mean(kf[rows], axis=0, keepdims=True)
        mbvt_ref[0, t] = mb_vt[:, rows].astype(BF16)


def _in_proj(x2, g, w_main, w_vt, mla_params, batch, seq):
    n = batch * seq
    nk = seq // TILE
    per_b = seq // ROW_TILE
    tpr = ROW_TILE // TILE
    row = lambda cols: pl.BlockSpec((ROW_TILE, cols), lambda i: (i, 0))
    full = lambda shp: pl.BlockSpec(shp, lambda i: (0,) * len(shp))
    tab = pl.BlockSpec((ROW_TILE, MLA_HEAD_PAD), lambda i: (i % per_b, 0))
    head_spec = pl.BlockSpec((1, N_HEADS, ROW_TILE, MLA_HEAD_PAD), lambda i: (i // per_b, 0, i % per_b, 0))
    vt_spec = pl.BlockSpec((1, tpr, GROUP, TILE), lambda i: (i // per_b, i % per_b, 0, 0))
    km_spec = pl.BlockSpec((1, tpr, 1, GROUP), lambda i: (i // per_b, i % per_b, 0, 0))
    sd = jax.ShapeDtypeStruct
    heads = sd((batch, N_HEADS, seq, MLA_HEAD_PAD), BF16)
    vt_tiles = sd((batch, nk, GROUP, TILE), BF16)
    return pl.pallas_call(
        _in_proj_kernel,
        grid=(n // ROW_TILE,),
        in_specs=[row(D_MODEL), full((1, D_MODEL)), full((D_MODEL, MAIN_COLS)),
                  full((GROUP, D_MODEL)),
                  full((1, 256)), full((1, KV_RANK)),
                  full((256, N_HEADS * MLA_HEAD_PAD)),
                  full((KV_RANK, N_HEADS * MLA_HEAD_PAD)), full((GROUP, KV_RANK)),
                  tab, tab, tab, tab],
        out_specs=[head_spec, head_spec, vt_spec,
                   row(GROUP), row(GROUP), row(GROUP), row(GROUP), row(GROUP),
                   row(2 * GROUP), row(D_MIX), vt_spec, vt_spec, km_spec],
        out_shape=[heads, heads, vt_tiles,
                   sd((n, GROUP), BF16), sd((n, GROUP), BF16),
                   sd((n, GROUP), BF16), sd((n, GROUP), F32), sd((n, GROUP), BF16),
                   sd((n, 2 * GROUP), F32), sd((n, D_MIX), BF16),
                   vt_tiles, vt_tiles, sd((batch, nk, 1, GROUP), F32)],
        scratch_shapes=[pltpu.VMEM((2 * GROUP // LANES, ROW_TILE, LANES), F32),
                        pltpu.VMEM((ROW_TILE, 2 * GROUP), F32)],
        compiler_params=_params(1),
        name="in_proj",
    )(x2, g, w_main, w_vt, *mla_params)


def _mla_prep(u, gq_ref, gkv_ref, wq_ref, wkn_ref, wvt_ref,
              cq_ref, sq_ref, ck_ref, sk_ref, q_ref, k_ref, vt_ref):
    cq = u[:, 0:256]
    lane = lax.broadcasted_iota(jnp.int32, cq.shape, 1)
    msq = jnp.sum(jnp.where(lane < Q_RANK, cq * cq, 0.0), axis=-1, keepdims=True) * (1.0 / Q_RANK)
    cqn = (cq * lax.rsqrt(msq + EPS) * gq_ref[...]).astype(BF16)
    qa = _dot(cqn, wq_ref[...])
    ckv = u[:, 256:384]
    msk = jnp.mean(ckv * ckv, axis=-1, keepdims=True)
    ckvn = (ckv * lax.rsqrt(msk + EPS) * gkv_ref[...]).astype(BF16)
    kn = _dot(ckvn, wkn_ref[...])
    tail = u[:, LANES:2 * LANES]
    kr = tail * ck_ref[...] + pltpu.roll(tail, LANES - ROPE_DIM, 1) * sk_ref[...]
    cq_t = cq_ref[...]
    sq_t = sq_ref[...]
    first_half = lax.broadcasted_iota(jnp.int32, cq_t.shape, 1) < HEAD_DIM + ROPE_HALF
    for h in range(N_HEADS):
        sl = slice(h * MLA_HEAD_PAD, (h + 1) * MLA_HEAD_PAD)
        qh = qa[:, sl]
        swapped = jnp.where(first_half, pltpu.roll(qh, LANES - ROPE_HALF, 1), pltpu.roll(qh, ROPE_HALF, 1))
        q_ref[0, h] = (qh * cq_t + swapped * sq_t).astype(BF16)
        k_ref[0, h] = (kn[:, sl] + kr).astype(BF16)
    vt = _dot_nt(wvt_ref[...], ckvn)
    for t in range(ROW_TILE // TILE):
        vt_ref[0, t] = vt[:, t * TILE:(t + 1) * TILE].astype(BF16)


def _head_rows(h):
    return slice(h * HEAD_DIM, (h + 1) * HEAD_DIM)


def _head_cols(h):
    return slice(h * TILE, (h + 1) * TILE)


def _key_rows(j):
    return pl.ds(pl.multiple_of(j * TILE, TILE), TILE)


def _store_head_masked_q(q, qbd_ref):
    lane = lax.broadcasted_iota(jnp.int32, (TILE, GROUP), 1)
    for h in range(N_HEADS):
        in_head = (lane >= h * HEAD_DIM) & (lane < (h + 1) * HEAD_DIM)
        qbd_ref[h * TILE:(h + 1) * TILE, :] = jnp.where(in_head, q, jnp.zeros_like(q))


def _sweep_scratch():
    s_buf = pltpu.VMEM((TILE, N_HEADS * TILE), F32)
    p_buf = pltpu.VMEM((N_HEADS, TILE, TILE), BF16)
    stat = pltpu.VMEM((N_HEADS, 1, TILE), F32)
    return [s_buf, s_buf, p_buf, p_buf, stat, stat, stat, pltpu.VMEM((N_HEADS, ACC_ROWS, TILE), F32)]


def _softmax_sweep(qi, score_fn, adjust_fn, vt_ref, o_ref, scratch, pv_first):
    s_a, s_b, p_a, p_b, al_a, al_b, m_ref, acc_ref = scratch
    row = lax.broadcasted_iota(jnp.int32, (TILE, TILE), 0)
    col = lax.broadcasted_iota(jnp.int32, (TILE, TILE), 1)
    ones_rows = jnp.ones((ONES_ROWS, TILE), BF16)

    def key_tile(t):
        if isinstance(t, int) and t == 0:
            return qi
        return jnp.minimum(t - 1, jnp.maximum(qi - 1, 0))

    def softmax(h, t, s_ref, p_ref, al_ref):
        first = isinstance(t, int) and t == 0
        s, pen = adjust_fn(h, key_tile(t), s_ref[:, _head_cols(h)], first)
        if first:
            s = jnp.where(row <= col, s, -jnp.inf)
        s = s.astype(BF16)
        m_tile = jnp.max(s, axis=0, keepdims=True).astype(F32)
        if first:
            m_new = m_tile
            p = jnp.exp2(s - m_new.astype(BF16))
        else:
            dead = jnp.where(t <= qi, 0.0, -jnp.inf)
            pen = dead if pen is None else pen + dead
            m_old = m_ref[h]
            m_new = jnp.maximum(m_old, m_tile + pen)
            al_ref[h] = jnp.exp2(m_old - m_new)
            p = jnp.exp2(s - (m_new - pen).astype(BF16))
        m_ref[h] = m_new
        p_ref[h] = p

    def weighted_values(h, t, p_ref, al_ref):
        first = isinstance(t, int) and t == 0
        vt = jnp.concatenate([vt_ref[0, key_tile(t), _head_rows(h), :], ones_rows], axis=0)
        pv = _dot(vt, p_ref[h])[:ACC_ROWS]
        acc_ref[h] = pv if first else acc_ref[h] * al_ref[h] + pv

    def half(t, cur, nxt):
        (s_cur, p_cur, al_cur), (s_nxt, p_nxt, al_nxt) = cur, nxt
        for h in range(N_HEADS):
            if not pv_first:
                score_fn(h, key_tile(t + 1), s_nxt)
            if not (isinstance(t, int) and t == 0):
                weighted_values(h, t - 1, p_nxt, al_nxt)
            if pv_first:
                score_fn(h, key_tile(t + 1), s_nxt)
            softmax(h, t, s_cur, p_cur, al_cur)

    buf_a, buf_b = (s_a, p_a, al_a), (s_b, p_b, al_b)
    for h in range(N_HEADS):
        score_fn(h, key_tile(0), s_a)
    half(0, buf_a, buf_b)
    half(1, buf_b, buf_a)

    def pair(k, carry):
        half(2 * k, buf_a, buf_b)
        half(2 * k + 1, buf_b, buf_a)
        return carry

    def pairs_from(first, count):
        def body(k, carry):
            for i in range(count):
                pair(first + count * k + i, carry)
            return carry
        return body

    n_pairs = qi // 2 + 1
    n_quad = (n_pairs - 1) // 4
    n_double = ((n_pairs - 1) % 4) // 2
    lax.fori_loop(0, n_quad, pairs_from(1, 4), 0)
    lax.fori_loop(0, n_double, pairs_from(4 * n_quad + 1, 2), 0)
    lax.fori_loop(4 * n_quad + 2 * n_double + 1, n_pairs, pair, 0)
    for h in range(N_HEADS):
        weighted_values(h, 2 * n_pairs - 1, p_b, al_b)
    out_t = [acc_ref[h, :HEAD_DIM, :] / acc_ref[h, HEAD_DIM:HEAD_DIM + 1, :] for h in range(N_HEADS)]
    o_ref[0] = jnp.concatenate(out_t, axis=0).T.astype(o_ref.dtype)


def _mla_attn_kernel(q_ref, k_ref, vt_ref, o_ref, *scratch):
    qi = pl.program_id(1)

    def score_fn(h, j, s_ref):
        s_ref[:, _head_cols(h)] = _dot_nt(k_ref[0, h, _key_rows(j), :], q_ref[0, h])

    _softmax_sweep(qi, score_fn, lambda h, j, s, first: (s, None), vt_ref, o_ref, scratch, pv_first=False)


def _mla_attn(q, k, vt, batch, seq):
    nk = seq // TILE
    return pl.pallas_call(
        _mla_attn_kernel,
        grid=(batch, nk),
        in_specs=[pl.BlockSpec((1, N_HEADS, TILE, MLA_HEAD_PAD), lambda b, i: (b, 0, i, 0)),
                  pl.BlockSpec((1, N_HEADS, seq, MLA_HEAD_PAD), lambda b, i: (b, 0, 0, 0)),
                  pl.BlockSpec((1, nk, GROUP, TILE), lambda b, i: (b, 0, 0, 0))],
        out_specs=pl.BlockSpec((1, TILE, GROUP), lambda b, i: (b, i, 0)),
        out_shape=jax.ShapeDtypeStruct((batch, seq, GROUP), BF16),
        scratch_shapes=_sweep_scratch(),
        compiler_params=_params(2),
        name="mla_attn",
    )(q, k, vt)


def _split_bf16(x):
    hi = x.astype(BF16)
    lo = (x - hi.astype(F32)).astype(BF16)
    return hi, lo


def _suffix_scan(x, carry):
    slabs = [x[i * 8:(i + 1) * 8, :] for i in range(SCAN_RUN)]
    rest_of_run = [None] * SCAN_RUN
    rest_of_run[SCAN_RUN - 1] = slabs[SCAN_RUN - 1]
    for i in range(SCAN_RUN - 2, -1, -1):
        rest_of_run[i] = rest_of_run[i + 1] + slabs[i]
    run_total = rest_of_run[0]
    sub = lax.broadcasted_iota(jnp.int32, run_total.shape, 0)
    from_here = run_total
    for d in (1, 2, 4):
        from_here = from_here + jnp.where(sub + d < 8, pltpu.roll(from_here, 8 - d, 0), 0.0)
    later_runs = from_here - run_total + carry
    suffix = jnp.concatenate([rest_of_run[i] + later_runs for i in range(SCAN_RUN)], axis=0)
    return suffix, from_here[0:1, :]


def _sb_attn_kernel(q_ref, k_ref, vt_ref, o_ref, qbd_ref, carry_ref, acc_ref):
    qi = pl.program_id(1)
    _store_head_masked_q(q_ref[0], qbd_ref)
    row = lax.broadcasted_iota(jnp.int32, (TILE, TILE), 0)
    col = lax.broadcasted_iota(jnp.int32, (TILE, TILE), 1)
    token = (row & 7) * SCAN_RUN + (row >> 3)
    past = token < col

    def scores(j):
        return _dot_nt(k_ref[0, _key_rows(j), :], qbd_ref[...])

    def finish(j, z_all, diag):
        for h in range(N_HEADS):
            hs = _head_rows(h)
            z = z_all[:, h * TILE:(h + 1) * TILE]
            neg_abs = lax.bitcast_convert_type(
                lax.bitcast_convert_type(z, jnp.uint32) | jnp.uint32(SIGN_BIT), F32)
            w = jnp.maximum(z, 0.0) + jnp.log2(1.0 + jnp.exp2(neg_abs))
            if diag:
                w = jnp.where(past, w, 0.0)
                w_from_here, tile_sum = _suffix_scan(w, jnp.zeros((1, TILE), F32))
                a = jnp.where(past, jnp.exp2(z - w_from_here), 0.0)
                carry_ref[h] = tile_sum
                acc_ref[hs, :] = _dot(vt_ref[0, j, hs, :], a.astype(BF16))
            else:
                carry = carry_ref[h]
                w_from_here, tile_sum = _suffix_scan(w, carry)
                a = jnp.exp2(z - w_from_here)
                carry_ref[h] = carry + tile_sum
                acc_ref[hs, :] = acc_ref[hs, :] + _dot(vt_ref[0, j, hs, :], a.astype(BF16))

    def live():
        return jnp.min(carry_ref[...]) < -SB_EXP2_UNDERFLOW

    @pl.when(qi == 0)
    def _():
        finish(qi, scores(qi), True)

    @pl.when(qi > 0)
    def _():
        z_diag, z_prev = scores(qi), scores(qi - 1)
        finish(qi, z_diag, True)
        finish(qi - 1, z_prev, False)

    def body(c):
        j = qi - 1 - c[0]
        finish(j, scores(j), False)
        return c[0] + 1, live()

    lax.while_loop(lambda c: (c[0] < qi) & c[1], body, (jnp.minimum(qi, 1), live()))
    o_ref[0] = acc_ref[...].T.astype(o_ref.dtype)


def _sb_attn(q, k, vt, batch, seq):
    nk = seq // TILE
    return pl.pallas_call(
        _sb_attn_kernel,
        grid=(batch, nk),
        in_specs=[pl.BlockSpec((1, TILE, GROUP), lambda b, i: (b, i, 0)),
                  pl.BlockSpec((1, seq, GROUP), lambda b, i: (b, 0, 0)),
                  pl.BlockSpec((1, nk, GROUP, TILE), lambda b, i: (b, 0, 0, 0))],
        out_specs=pl.BlockSpec((1, TILE, GROUP), lambda b, i: (b, i, 0)),
        out_shape=jax.ShapeDtypeStruct((batch, seq, GROUP), BF16),
        scratch_shapes=[pltpu.VMEM((N_HEADS * TILE, GROUP), BF16), pltpu.VMEM((N_HEADS, 1, TILE), F32),
                        pltpu.VMEM((GROUP, TILE), F32)],
        compiler_params=_params(2),
        name="sb_attn",
    )(q, k, vt)


def _moba_attn_kernel(nk, q_ref, qf_ref, k_ref, vt_ref, km_ref, bias_ref, o_ref,
                      qbd_ref, sel_ref, *scratch):
    qi = pl.program_id(1)
    _store_head_masked_q(q_ref[0], qbd_ref)
    km = km_ref[0, :, 0, :]
    lane = lax.broadcasted_iota(jnp.int32, (nk, GROUP), 1)
    km_hi, km_lo = _split_bf16(jnp.concatenate(
        [jnp.where((lane >= h * HEAD_DIM) & (lane < (h + 1) * HEAD_DIM), km, 0.0) for h in range(N_HEADS)], axis=0))
    qf_hi, qf_lo = _split_bf16(qf_ref[0])
    gate_all = _dot_nt(km_hi, qf_hi) + (_dot_nt(km_hi, qf_lo) + _dot_nt(km_lo, qf_hi))
    blk = lax.broadcasted_iota(jnp.int32, (nk, TILE), 0)
    for h in range(N_HEADS):
        gate = gate_all[h * nk:(h + 1) * nk, :]
        rival = jnp.where(blk < qi, gate, -jnp.inf)
        beaten = jnp.zeros((nk, TILE), jnp.int32)
        for jp in range(nk):
            gj = rival[jp:jp + 1, :]
            beaten = beaten + jnp.where((gj > gate) | ((gj == gate) & (jp < blk)), 1, 0)
        keep = (blk < qi) & (beaten < MOBA_TOPK)
        sel_ref[h] = jnp.where(keep, 0.0, -jnp.inf)

    def score_fn(h, j, s_ref):
        s_ref[:, _head_cols(h)] = _dot_nt(k_ref[0, _key_rows(j), :], qbd_ref[_head_cols(h), :])

    def adjust_fn(h, j, s, first):
        if first:
            return s + bias_ref[0, h], None
        d = jnp.minimum(qi - j, N_BIAS_TILES - 1)
        return s + bias_ref[d, h], sel_ref[h, pl.ds(j, 1), :]

    _softmax_sweep(qi, score_fn, adjust_fn, vt_ref, o_ref, scratch, pv_first=True)


def _moba_attn(q, qf, k, vt, kmean, bias, batch, seq):
    nk = seq // TILE
    return pl.pallas_call(
        functools.partial(_moba_attn_kernel, nk),
        grid=(batch, nk),
        in_specs=[pl.BlockSpec((1, TILE, GROUP), lambda b, i: (b, i, 0)),
                  pl.BlockSpec((1, TILE, GROUP), lambda b, i: (b, i, 0)),
                  pl.BlockSpec((1, seq, GROUP), lambda b, i: (b, 0, 0)),
                  pl.BlockSpec((1, nk, GROUP, TILE), lambda b, i: (b, 0, 0, 0)),
                  pl.BlockSpec((1, nk, 1, GROUP), lambda b, i: (b, 0, 0, 0)),
                  pl.BlockSpec((N_BIAS_TILES, N_HEADS, TILE, TILE), lambda b, i: (0, 0, 0, 0))],
        out_specs=pl.BlockSpec((1, TILE, GROUP), lambda b, i: (b, i, 0)),
        out_shape=jax.ShapeDtypeStruct((batch, seq, GROUP), BF16),
        scratch_shapes=[pltpu.VMEM((N_HEADS * TILE, GROUP), BF16), pltpu.VMEM((N_HEADS, nk, TILE), F32)]
        + _sweep_scratch(),
        compiler_params=_params(2),
        name="moba_attn",
    )(q, qf, k, vt, kmean, bias)


def _conv_module(i, u_ref, halo_ref, dw_ref, dwb_ref, lng_ref, lnb_ref, pw_ref, pwb_ref, ext_ref, shift_ref):
    um = u_ref[0]
    uh = halo_ref[0]
    xh = uh[:, :GROUP] * jax.nn.sigmoid(uh[:, GROUP:])
    ext_ref[0:CONV_HALO, :] = jnp.where(i == 0, 0.0, xh)
    ext_ref[CONV_HALO:, :] = um[:, :GROUP] * jax.nn.sigmoid(um[:, GROUP:])
    base = CONV_HALO - (CONV_WIDTH - 1)
    for b in range(1, 8):
        shift_ref[b - 1] = ext_ref[b:b + CONV_SHIFT_ROWS, :]
    y = jnp.zeros((ROW_TILE, GROUP), F32)
    for w in range(CONV_WIDTH):
        a, b = divmod(base + w, 8)
        rows = slice(8 * a, 8 * a + ROW_TILE)
        tap = ext_ref[rows, :] if b == 0 else shift_ref[b - 1, rows, :]
        y = y + tap * dw_ref[w:w + 1, :]
    y = y + dwb_ref[...]
    mu = jnp.mean(y, axis=-1, keepdims=True)
    yc = y - mu
    var = jnp.mean(yc * yc, axis=-1, keepdims=True)
    yn = yc * lax.rsqrt(var + EPS) * lng_ref[...] + lnb_ref[...]
    sw = yn * jax.nn.sigmoid(yn)
    return _dot(sw.astype(BF16), pw_ref[...]) + pwb_ref[...]


def _out_proj_kernel(u_ref, halo_ref, dw_ref, dwb_ref, lng_ref, lnb_ref, pw_ref, pwb_ref,
                     oa_ref, ob_ref, oc_ref, gate_ref, w_ref, g_ref, x_ref, o_ref, ext_ref, shift_ref):
    od = _conv_module(pl.program_id(1), u_ref, halo_ref, dw_ref, dwb_ref, lng_ref, lnb_ref, pw_ref, pwb_ref,
                      ext_ref, shift_ref)
    sg = gate_ref[0].astype(F32)
    mix =jnp.concatenate([oa_ref[0].astype(F32), ob_ref[0].astype(F32), oc_ref[0].astype(F32), od], axis=-1)
    y = _dot((mix * sg).astype(BF16), w_ref[...])
    ms = jnp.mean(y * y, axis=-1, keepdims=True)
    o_ref[0] = x_ref[0] + y * lax.rsqrt(ms + EPS) * g_ref[...]


def _out_proj(u_conv, conv_params, oa, ob, oc, gate, w, g, x3):
    batch, seq, _ = x3.shape
    per_b = seq // ROW_TILE
    halo_per_tile = ROW_TILE // CONV_HALO
    row = lambda cols: pl.BlockSpec((1, ROW_TILE, cols), lambda b, i: (b, i, 0))
    full = lambda shp: pl.BlockSpec(shp, lambda b, i: (0,) * len(shp))
    return pl.pallas_call(
        _out_proj_kernel,
        grid=(batch, per_b),
        in_specs=[row(2 * GROUP),
                  pl.BlockSpec((1, CONV_HALO, 2 * GROUP),
                               lambda b, i: (b, jnp.maximum(i * halo_per_tile - 1, 0), 0)),
                  full((CONV_HALO, GROUP)), full((1, GROUP)), full((1, GROUP)), full((1, GROUP)),
                  full((GROUP, GROUP)), full((1, GROUP)),
                  row(GROUP), row(GROUP), row(GROUP), row(D_MIX),
                  full((D_MIX, D_MODEL)), full((1, D_MODEL)), row(D_MODEL)],
        out_specs=row(D_MODEL),
        out_shape=jax.ShapeDtypeStruct((batch, seq, D_MODEL), F32),
        scratch_shapes=[pltpu.VMEM((CONV_HALO + ROW_TILE, GROUP), F32),
                        pltpu.VMEM((7, CONV_SHIFT_ROWS, GROUP), F32)],
        compiler_params=_params(2),
        name="out_proj",
    )(u_conv, u_conv, *conv_params, oa, ob, oc, gate, w, g, x3)


def _prep_in_proj_weights(w_in):
    c = 0
    cq = w_in[..., c:c + Q_RANK]; c += Q_RANK
    ckv = w_in[..., c:c + KV_RANK]; c += KV_RANK
    kr = w_in[..., c:c + ROPE_DIM]; c += ROPE_DIM
    sbq, sbk, sbv = (w_in[..., c + i * GROUP:c + (i + 1) * GROUP] for i in range(3)); c += 3 * GROUP
    mbq, mbk, mbv = (w_in[..., c + i * GROUP:c + (i + 1) * GROUP] for i in range(3)); c += 3 * GROUP
    conv = w_in[..., c:c + 2 * GROUP]; c += 2 * GROUP
    gate = w_in[..., c:c + D_MIX]
    kr_swapped = jnp.concatenate([kr[..., ROPE_HALF:], kr[..., :ROPE_HALF]], axis=-1)
    scale = HEAD_DIM ** -0.5
    main = jnp.concatenate([cq, kr, kr_swapped, ckv, sbq * scale, sbk, sbv, mbq * scale, mbk, conv, gate], axis=-1)
    return main.astype(BF16), jnp.swapaxes(mbv, 1, 2).astype(BF16)


def _prep_mla_weights(w_uq, w_ukv):
    depth = w_uq.shape[0]
    qh = w_uq.reshape(depth, Q_RANK, N_HEADS, HEAD_DIM + ROPE_DIM)
    nope, r1, r2 = qh[..., :HEAD_DIM], qh[..., HEAD_DIM:HEAD_DIM + ROPE_HALF], qh[..., HEAD_DIM + ROPE_HALF:]
    zpad = jnp.zeros((depth, Q_RANK, N_HEADS, MLA_HEAD_PAD - HEAD_DIM - ROPE_DIM), w_uq.dtype)
    row_pad = ((0, 0), (0, 256 - Q_RANK), (0, 0))
    wq = jnp.pad(jnp.concatenate([nope, r1, r2, zpad], -1).reshape(depth, Q_RANK, -1), row_pad)
    kvh = w_ukv.reshape(depth, KV_RANK, N_HEADS, 2 * HEAD_DIM)
    k_nope, v = kvh[..., :HEAD_DIM], kvh[..., HEAD_DIM:]
    wkn = jnp.concatenate([k_nope, jnp.zeros_like(k_nope)], -1).reshape(depth, KV_RANK, -1)
    wvt = jnp.swapaxes(v.reshape(depth, KV_RANK, GROUP), 1, 2)
    return wq.astype(BF16), wkn.astype(BF16), wvt.astype(BF16)


def _rope_tables(seq):
    freqs = ROPE_THETA ** (-jnp.arange(ROPE_HALF, dtype=F32) / ROPE_HALF)
    ang = jnp.arange(seq, dtype=jnp.int32).astype(F32)[:, None] * freqs[None, :]
    cos, sin = jnp.cos(ang), jnp.sin(ang)
    ones, zeros = jnp.ones((seq, HEAD_DIM), F32), jnp.zeros((seq, HEAD_DIM), F32)
    ztail = jnp.zeros((seq, MLA_HEAD_PAD - HEAD_DIM - ROPE_DIM), F32)
    c_tab = jnp.concatenate([ones, cos, cos, ztail], axis=-1)
    s_tab = jnp.concatenate([zeros, -sin, sin, ztail], axis=-1)
    ck_tab = jnp.concatenate([zeros, cos, cos, ztail], axis=-1)
    return c_tab, s_tab, ck_tab


def kernel(x, pre_norm_g, w_in, mla_q_norm_g, mla_w_uq, mla_kv_norm_g, mla_w_ukv, rel_bias, conv_dw_w, conv_dw_b, conv_ln_g, conv_ln_b, conv_pw_w, conv_pw_b, w_out, post_norm_g):
    batch, seq, d_model = x.shape
    depth = w_in.shape[0]
    assert d_model == D_MODEL and seq % ROW_TILE == 0 and ROW_TILE % TILE == 0
    n = batch * seq

    w_main, w_vt = _prep_in_proj_weights(w_in)
    wq, wkn, wvt_mla = _prep_mla_weights(mla_w_uq, mla_w_ukv)
    c_tab, s_tab, ck_tab = _rope_tables(seq)
    mla_scale = (HEAD_DIM + ROPE_DIM) ** -0.5 * LOG2E
    cq_tab, sq_tab = c_tab * mla_scale, s_tab * mla_scale
    gq = jnp.pad(mla_q_norm_g, ((0, 0), (0, 256 - Q_RANK)))
    dw = jnp.pad(conv_dw_w, ((0, 0), (0, CONV_HALO - CONV_WIDTH), (0, 0)))
    pw = conv_pw_w.astype(BF16)
    w_o = w_out.astype(BF16)
    bias = _bias_tiles(rel_bias)

    seq_major = lambda a: a.reshape(batch, seq, a.shape[-1])
    for l in range(depth):
        mla_params = (gq[l][None], mla_kv_norm_g[l][None], wq[l], wkn[l], wvt_mla[l],
                      cq_tab, sq_tab, ck_tab, s_tab)
        (qcat, kcat, mla_vt, sbq, sbk, mbq, mbqf, mbk, u_conv, u_gate, sbvt, mbvt, kmean) = _in_proj(
            x.reshape(n, D_MODEL), pre_norm_g[l][None], w_main[l], w_vt[l], mla_params, batch, seq)
        o_a = _mla_attn(qcat, kcat, mla_vt, batch, seq)
        o_b = _sb_attn(seq_major(sbq), seq_major(sbk), sbvt, batch, seq)
        o_c = _moba_attn(seq_major(mbq), seq_major(mbqf), seq_major(mbk), mbvt, kmean, bias, batch, seq)
        conv_params = (dw[l], conv_dw_b[l][None], conv_ln_g[l][None], conv_ln_b[l][None], pw[l],
                       conv_pw_b[l][None])
        x = _out_proj(seq_major(u_conv), conv_params, o_a, o_b, o_c, seq_major(u_gate), w_o[l],
                      post_norm_g[l][None], x)
    return x
```
